```python
import math
import jax, jax.numpy as jnp
from jax import lax
import numpy as np

D_MODEL = 1024
BATCH = 1
SEQ = 16384
DEPTH = 1
DEC_BATCH = 32
DEC_SEQ = 1
PAST_LEN = 16384
PAGE_SIZE = 128

ATT_HEADS = 8
HEAD_DIM = 64
ATT_W = ATT_HEADS * HEAD_DIM
IDX_HEADS = 8
IDX_DIM = 64
IDX_TOPK_MAX = 256
Q_BLOCK = 128
ROPE_THETA = 500000.0
ROPE_FRACTION = 4
SSM_GROUP = 16
SSM_W = 512
SSM_GROUPS = SSM_W // SSM_GROUP
SSM_STATE = 64
MEM_TOKENS = 256
MEM_HEADS = 4
MEM_HD = 128
MEM_W = MEM_HEADS * MEM_HD
N_BRANCH = 3
PEER_HEADS = 8
N_KEYS = 128
N_EXPERTS = N_KEYS * N_KEYS
PEER_DK = 128
PEER_TOPK = 16
PEER_BLOCK = 128
EPS = 1e-6

IN_SPLITS = [ATT_W, ATT_W, ATT_W, IDX_HEADS * IDX_DIM, IDX_DIM, IDX_HEADS, SSM_W, MEM_W, N_BRANCH * D_MODEL]
IN_COLS = sum(IN_SPLITS)

kernel_name = 'dsa_s5_mem_peer_hybrid_step'


def rmsnorm(x, g):
    xf = x.astype(jnp.float32)
    xf = xf * lax.rsqrt(jnp.mean(xf * xf, axis=-1, keepdims=True) + EPS)
    return (xf * g.astype(jnp.float32)).astype(x.dtype)


def rope(x, pos):
    rd = x.shape[-1] // ROPE_FRACTION
    half = rd // 2
    inv_freq = ROPE_THETA ** (-jnp.arange(half, dtype=jnp.float32) / half)
    ang = pos.astype(jnp.float32)[:, None] * inv_freq[None, :]
    cos = jnp.cos(ang)[:, None, :]
    sin = jnp.sin(ang)[:, None, :]
    xf = x.astype(jnp.float32)
    x1, x2 = xf[..., :half], xf[..., half:rd]
    out = jnp.concatenate([x1 * cos - x2 * sin, x1 * sin + x2 * cos, xf[..., rd:]], axis=-1)
    return out.astype(x.dtype)


def mixer_project(x, pos, norm_mix, w_in, q_norm, k_norm, idx_k_norm, mq_norm):
    b, t, _ = x.shape
    xn = rmsnorm(x, norm_mix)
    splits = np.cumsum(IN_SPLITS)[:-1].tolist()
    q, k, v, qi, ki, wi, u, qm, gate = jnp.split(xn @ w_in, splits, axis=-1)
    q = rope(rmsnorm(q.reshape(b, t, ATT_HEADS, HEAD_DIM), q_norm), pos)
    k = rope(rmsnorm(k.reshape(b, t, ATT_HEADS, HEAD_DIM), k_norm), pos)
    v = v.reshape(b, t, ATT_HEADS, HEAD_DIM)
    qi = rope(qi.reshape(b, t, IDX_HEADS, IDX_DIM), pos)
    ki = rope(rmsnorm(ki, idx_k_norm)[:, :, None, :], pos)[:, :, 0, :]
    wi = wi * IDX_HEADS ** -0.5
    qm = rmsnorm(qm.reshape(b, t, MEM_HEADS, MEM_HD), mq_norm)
    gate = jax.nn.sigmoid(gate.astype(jnp.float32)).astype(x.dtype).reshape(b, t, N_BRANCH, D_MODEL)
    return q, k, v, qi, ki, wi, u, qm, gate


def indexer_topk(qi, ki, wi, pos_q, topk):
    s = jnp.einsum('bthd,bld->bthl', qi, ki).astype(jnp.float32) * IDX_DIM ** -0.5
    score = jnp.einsum('bth,bthl->btl', wi.astype(jnp.float32), jax.nn.relu(s))
    causal = jnp.arange(ki.shape[1])[None, :] <= pos_q[:, None]
    score = jnp.where(causal[None], score, -jnp.inf)
    _, sel = lax.top_k(score, topk)
    return sel, sel <= pos_q[None, :, None]


def sparse_attend(q, k_sel, v_sel, valid):
    s = jnp.einsum('bthd,btkhd->bthk', q, k_sel).astype(jnp.float32) * HEAD_DIM ** -0.5
    s = jnp.where(valid[:, :, None, :], s, -jnp.inf)
    p = jax.nn.softmax(s, axis=-1).astype(v_sel.dtype)
    return jnp.einsum('bthk,btkhd->bthd', p, v_sel)


def attn_prompt(q, k, v, qi, ki, wi):
    b, s = q.shape[:2]
    topk = min(IDX_TOPK_MAX, s // 4)
    b_ix = jnp.arange(b)[:, None, None]

    def block(j):
        t0 = j * Q_BLOCK
        sl = lambda a: lax.dynamic_slice_in_dim(a, t0, Q_BLOCK, axis=1)
        pos_q = t0 + jnp.arange(Q_BLOCK)
        sel, valid = indexer_topk(sl(qi), ki, sl(wi), pos_q, topk)
        return sparse_attend(sl(q), k[b_ix, sel], v[b_ix, sel], valid)

    out = lax.map(block, jnp.arange(s // Q_BLOCK))
    return out.transpose(1, 0, 2, 3, 4).reshape(b, s, ATT_W)


def attn_sample(q, k, v, qi, ki, wi, cache_k, cache_v, cache_idx_k, page_table):
    bd, t = q.shape[:2]
    n_pages = page_table.shape[1]
    topk = min(IDX_TOPK_MAX, (PAST_LEN + t) // 4)
    ki_past = cache_idx_k[page_table].reshape(bd, n_pages * PAGE_SIZE, IDX_DIM)
    ki_all = jnp.concatenate([ki_past, ki], axis=1)
    pos_q = PAST_LEN + jnp.arange(t)
    sel, valid = indexer_topk(qi, ki_all, wi, pos_q, topk)
    b_ix = jnp.arange(bd)[:, None, None]
    in_past = (sel < PAST_LEN)[..., None, None]
    sp = jnp.minimum(sel, PAST_LEN - 1)
    phys = page_table[b_ix, sp // PAGE_SIZE]
    slot = sp % PAGE_SIZE
    sn = jnp.clip(sel - PAST_LEN, 0, t - 1)
    k_sel = jnp.where(in_past, cache_k[phys, slot], k[b_ix, sn])
    v_sel = jnp.where(in_past, cache_v[phys, slot], v[b_ix, sn])
    return sparse_attend(q, k_sel, v_sel, valid).reshape(bd, t, ATT_W)


def _ssm_combine(left, right):
    a_l, h_l = left
    a_r, h_r = right
    return a_l * a_r, a_r * h_l + h_r


def ssm_branch(u, h0, a_re, a_im, b_re, b_im, c_re, c_im, d, log_dt, w_glu):
    b, t, _ = u.shape
    f32 = jnp.float32
    uf = u.astype(f32).reshape(b, t, SSM_GROUPS, SSM_GROUP)
    a = lax.complex(a_re.astype(f32), a_im.astype(f32))
    dt = jnp.exp(log_dt.astype(f32))[:, None]
    a_bar = jnp.exp(a * dt)
    b_bar = ((a_bar - 1.0) / a)[:, :, None] * lax.complex(b_re.astype(f32), b_im.astype(f32))
    bu = jnp.einsum('gpc,btgc->btgp', b_bar, uf.astype(jnp.complex64))
    a_cum, h = lax.associative_scan(_ssm_combine, (jnp.broadcast_to(a_bar, bu.shape), bu), axis=1)
    if h0 is not None:
        h = h + a_cum * h0[:, None]
    c = lax.complex(c_re.astype(f32), c_im.astype(f32))
    y = jnp.einsum('gcp,btgp->btgc', c, h).real + d.astype(f32) * uf
    z = jax.nn.gelu(y.reshape(b, t, SSM_W))
    out = z * jax.nn.sigmoid(z @ w_glu.astype(f32))
    return out.astype(u.dtype), h[:, -1]


def mem_kv(mem, mem_norm, w_mem_kv, mk_norm):
    b, m, _ = mem.shape
    mk, mv = jnp.split(rmsnorm(mem, mem_norm) @ w_mem_kv, 2, axis=-1)
    return rmsnorm(mk.reshape(b, m, MEM_HEADS, MEM_HD), mk_norm), mv.reshape(b, m, MEM_HEADS, MEM_HD)


def mem_attend(qm, mk, mv):
    b, t = qm.shape[:2]
    s = jnp.einsum('bthd,bmhd->bthm', qm, mk).astype(jnp.float32) * MEM_HD ** -0.5
    p = jax.nn.softmax(s, axis=-1).astype(mv.dtype)
    return jnp.einsum('bthm,bmhd->bthd', p, mv).reshape(b, t, MEM_W)


def peer(xn, w_q, sub_keys, u_tab, v_tab):
    shape = xn.shape
    xt = xn.reshape(-1, D_MODEL)
    n = xt.shape[0]
    n_pad = -(-n // PEER_BLOCK) * PEER_BLOCK
    xt = jnp.pad(xt, ((0, n_pad - n), (0, 0)))
    n_cand = PEER_TOPK * PEER_TOPK

    def block(xb):
        q = (xb @ w_q).reshape(PEER_BLOCK, PEER_HEADS, 2, PEER_DK // 2)
        s = jnp.einsum('thid,hind->thin', q, sub_keys).astype(jnp.float32)
        s_half, i_half = lax.top_k(s, PEER_TOPK)
        cand = (s_half[:, :, 0, :, None] + s_half[:, :, 1, None, :]).reshape(PEER_BLOCK, PEER_HEADS, n_cand)
        cand_idx = (i_half[:, :, 0, :, None] * N_KEYS + i_half[:, :, 1, None, :]).reshape(PEER_BLOCK, PEER_HEADS, n_cand)
        top_s, top_pos = lax.top_k(cand, PEER_TOPK)
        expert = jnp.take_along_axis(cand_idx, top_pos, axis=-1)
        g = jax.nn.softmax(top_s, axis=-1)
        act = jax.nn.gelu(jnp.einsum('td,thkd->thk', xb, u_tab[expert]).astype(jnp.float32))
        return jnp.einsum('thk,thkd->td', (g * act).astype(xb.dtype), v_tab[expert])

    out = lax.map(block, xt.reshape(-1, PEER_BLOCK, D_MODEL))
    return out.reshape(n_pad, D_MODEL)[:n].reshape(shape)


def merge_and_channel_mix(x, gate, att_o, ssm_o, mem_o, w_br_attn, w_br_ssm, w_br_mem, w_out,
                          norm_ffn, peer_w_q, peer_sub_keys, peer_u, peer_v):
    mixed = (gate[:, :, 0] * (att_o @ w_br_attn)
             + gate[:, :, 1] * (ssm_o @ w_br_ssm)
             + gate[:, :, 2] * (mem_o @ w_br_mem))
    h = x + mixed @ w_out
    return h + peer(rmsnorm(h, norm_ffn), peer_w_q, peer_sub_keys, peer_u, peer_v)


def setup_inputs(seed: int = 0) -> dict:
    key = jax.random.key(seed)
    keys = list(jax.random.split(key, 48))
    f32 = jnp.float32
    nrm = lambda shape, scale: jax.random.normal(keys.pop(), shape, f32) * scale
    gain = lambda n: 1.0 + 0.02 * jax.random.normal(keys.pop(), (n,), f32)
    n_pages = PAST_LEN // PAGE_SIZE
    n_used = DEC_BATCH * n_pages
    n_phys = n_used + max(1, n_used // 4)
    page_table = jax.random.permutation(keys.pop(), n_phys)[:n_used].reshape(DEC_BATCH, n_pages).astype(jnp.int32)
    n_idx = jnp.arange(SSM_STATE, dtype=f32)[None, :]
    return {
        'x_prompt': nrm((BATCH, SEQ, D_MODEL), 1.0),
        'x_sample': nrm((DEC_BATCH, DEC_SEQ, D_MODEL), 1.0),
        'cache_k': nrm((n_phys, PAGE_SIZE, ATT_HEADS, HEAD_DIM), 1.0),
        'cache_v': nrm((n_phys, PAGE_SIZE, ATT_HEADS, HEAD_DIM), 1.0),
        'cache_idx_k': nrm((n_phys, PAGE_SIZE, IDX_DIM), 1.0),
        'cache_mem_k': nrm((DEC_BATCH, MEM_TOKENS, MEM_HEADS, MEM_HD), 1.0),
        'cache_mem_v': nrm((DEC_BATCH, MEM_TOKENS, MEM_HEADS, MEM_HD), 1.0),
        'state_ssm_re': nrm((DEC_BATCH, SSM_GROUPS, SSM_STATE), 0.5),
        'state_ssm_im': nrm((DEC_BATCH, SSM_GROUPS, SSM_STATE), 0.5),
        'page_table': page_table,
        'mem_prompt': nrm((BATCH, MEM_TOKENS, D_MODEL), 1.0),
        'norm_mix': gain(D_MODEL),
        'w_in': nrm((D_MODEL, IN_COLS), D_MODEL ** -0.5),
        'q_norm': gain(HEAD_DIM),
        'k_norm': gain(HEAD_DIM),
        'idx_k_norm': gain(IDX_DIM),
        'mq_norm': gain(MEM_HD),
        'ssm_a_re': -0.5 + nrm((SSM_GROUPS, SSM_STATE), 0.01),
        'ssm_a_im': math.pi * n_idx + nrm((SSM_GROUPS, SSM_STATE), 0.01),
        'ssm_b_re': nrm((SSM_GROUPS, SSM_STATE, SSM_GROUP), SSM_GROUP ** -0.5),
        'ssm_b_im': nrm((SSM_GROUPS, SSM_STATE, SSM_GROUP), SSM_GROUP ** -0.5),
        'ssm_c_re': nrm((SSM_GROUPS, SSM_GROUP, SSM_STATE), SSM_STATE ** -0.5),
        'ssm_c_im': nrm((SSM_GROUPS, SSM_GROUP, SSM_STATE), SSM_STATE ** -0.5),
        'ssm_d': nrm((SSM_GROUPS, SSM_GROUP), 1.0),
        'ssm_log_dt': jax.random.uniform(keys.pop(), (SSM_GROUPS,), f32, math.log(1e-3), math.log(1e-1)),
        'w_glu': nrm((SSM_W, SSM_W), SSM_W ** -0.5),
        'mem_norm': gain(D_MODEL),
        'w_mem_kv': nrm((D_MODEL, 2 * MEM_W), D_MODEL ** -0.5),
        'mk_norm': gain(MEM_HD),
        'w_br_attn': nrm((ATT_W, D_MODEL), ATT_W ** -0.5),
        'w_br_ssm': nrm((SSM_W, D_MODEL), SSM_W ** -0.5),
        'w_br_mem': nrm((MEM_W, D_MODEL), MEM_W ** -0.5),
        'w_out': nrm((D_MODEL, D_MODEL), D_MODEL ** -0.5),
        'norm_ffn': gain(D_MODEL),
        'peer_w_q': nrm((D_MODEL, PEER_HEADS * PEER_DK), D_MODEL ** -0.5),
        'peer_sub_keys': nrm((PEER_HEADS, 2, N_KEYS, PEER_DK // 2), (PEER_DK // 2) ** -0.5),
        'peer_u': nrm((N_EXPERTS, D_MODEL), D_MODEL ** -0.5),
        'peer_v': nrm((N_EXPERTS, D_MODEL), PEER_HEADS ** -0.5),
    }


def reference(x_prompt, x_sample, cache_k, cache_v, cache_idx_k, cache_mem_k, cache_mem_v,
              state_ssm_re, state_ssm_im, page_table, mem_prompt,
              norm_mix, w_in, q_norm, k_norm, idx_k_norm, mq_norm,
              ssm_a_re, ssm_a_im, ssm_b_re, ssm_b_im, ssm_c_re, ssm_c_im, ssm_d, ssm_log_dt, w_glu,
              mem_norm, w_mem_kv, mk_norm, w_br_attn, w_br_ssm, w_br_mem, w_out,
              norm_ffn, peer_w_q, peer_sub_keys, peer_u, peer_v):
    ssm_params = (ssm_a_re, ssm_a_im, ssm_b_re, ssm_b_im, ssm_c_re, ssm_c_im, ssm_d, ssm_log_dt, w_glu)
    tail_params = (w_br_attn, w_br_ssm, w_br_mem, w_out, norm_ffn, peer_w_q, peer_sub_keys, peer_u, peer_v)

    pos_p = jnp.arange(x_prompt.shape[1])
    q, k_prompt, v_prompt, qi, idx_k_prompt, wi, u, qm, gate = mixer_project(
        x_prompt, pos_p, norm_mix, w_in, q_norm, k_norm, idx_k_norm, mq_norm)
    att_o = attn_prompt(q, k_prompt, v_prompt, qi, idx_k_prompt, wi)
    ssm_o, h_last_p = ssm_branch(u, None, *ssm_params)
    mem_k_prompt, mem_v_prompt = mem_kv(mem_prompt, mem_norm, w_mem_kv, mk_norm)
    mem_o = mem_attend(qm, mem_k_prompt, mem_v_prompt)
    y_prompt = merge_and_channel_mix(x_prompt, gate, att_o, ssm_o, mem_o, *tail_params)

    pos_s = PAST_LEN + jnp.arange(x_sample.shape[1])
    q, k_sample, v_sample, qi, idx_k_sample, wi, u, qm, gate = mixer_project(
        x_sample, pos_s, norm_mix, w_in, q_norm, k_norm, idx_k_norm, mq_norm)
    att_o = attn_sample(q, k_sample, v_sample, qi, idx_k_sample, wi, cache_k, cache_v, cache_idx_k, page_table)
    h0 = lax.complex(state_ssm_re.astype(jnp.float32), state_ssm_im.astype(jnp.float32))
    ssm_o, h_last_s = ssm_branch(u, h0, *ssm_params)
    mem_o = mem_attend(qm, cache_mem_k, cache_mem_v)
    y_sample = merge_and_channel_mix(x_sample, gate, att_o, ssm_o, mem_o, *tail_params)

    return (y_prompt, y_sample,
            k_prompt, v_prompt, idx_k_prompt, mem_k_prompt, mem_v_prompt,
            h_last_p.real, h_last_p.imag,
            k_sample, v_sample, idx_k_sample,
            h_last_s.real, h_last_s.imag)
```

```python
import functools
import math

import jax
import jax.numpy as jnp
import numpy as np
from jax import lax
from jax.experimental import pallas as pl
from jax.experimental.pallas import tpu as pltpu

F32 = jnp.float32
BF16 = jnp.bfloat16
I32 = jnp.int32

D_MODEL = 1024
PAST_LEN = 16384
PAGE_SIZE = 128
ATT_HEADS = 8
HEAD_DIM = 64
ATT_W = ATT_HEADS * HEAD_DIM
IDX_HEADS = 8
IDX_DIM = 64
IDX_TOPK_MAX = 256
ROPE_THETA = 500000.0
ROPE_HALF = HEAD_DIM // 4 // 2
SSM_GROUP = 16
SSM_W = 512
SSM_GROUPS = SSM_W // SSM_GROUP
SSM_STATE = 64
SSM_LANES = SSM_GROUPS * SSM_STATE
MEM_TOKENS = 256
MEM_HEADS = 4
MEM_HD = 128
MEM_W = MEM_HEADS * MEM_HD
N_BRANCH = 3
PEER_HEADS = 8
N_KEYS = 128
N_EXPERTS = N_KEYS * N_KEYS
PEER_DK = 128
PEER_TOPK = 16
EPS = 1e-6

LANES = 128
SUBLANES = 8
VMEM_LIMIT = 56 * 1024 * 1024
INT_MIN = -(2 ** 31)
MASKED_SCORE = -1e30
RUNNING_MAX_INIT = -5e29

NT_DIMS = (((1,), (1,)), ((), ()))


def _params(*sem):
    return pltpu.CompilerParams(dimension_semantics=sem, vmem_limit_bytes=VMEM_LIMIT)


def _full(shape):
    nd = len(shape)
    return pl.BlockSpec(shape, lambda *_: (0,) * nd, pipeline_mode=pl.Buffered(1))


def _rms(x, g, n):
    ms = jnp.sum(x * x, axis=-1, keepdims=True) * (1.0 / n)
    return x * lax.rsqrt(ms + EPS) * g


def _dot(a, b):
    return jnp.dot(a, b, preferred_element_type=F32)


def _dot_nt(a, b):
    return lax.dot_general(a, b, NT_DIMS, preferred_element_type=F32)


def _mem_kv_kernel(mem_ref, mn_ref, w_ref, mkn_ref, mk_ref, mv_ref):
    xn = _rms(mem_ref[...], mn_ref[...], D_MODEL).astype(BF16)
    y = _dot(xn, w_ref[...])
    for h in range(MEM_HEADS):
        sl = slice(h * MEM_HD, (h + 1) * MEM_HD)
        mk_ref[:, sl] = _rms(y[:, sl], mkn_ref[...], MEM_HD)
    mv_ref[...] = y[:, MEM_W:]


def mem_kv(mem, mem_norm, w_mem_kv, mk_norm):
    m = mem.shape[0]
    return pl.pallas_call(
        _mem_kv_kernel,
        out_shape=(jax.ShapeDtypeStruct((m, MEM_W), F32), jax.ShapeDtypeStruct((m, MEM_W), F32)),
        compiler_params=pltpu.CompilerParams(vmem_limit_bytes=VMEM_LIMIT),
    )(mem, mem_norm.reshape(1, -1), w_mem_kv.astype(BF16), mk_norm.reshape(1, -1))


def _rope_tables(pos):
    inv_freq = ROPE_THETA ** (-jnp.arange(ROPE_HALF, dtype=F32) / ROPE_HALF)
    ang = pos.astype(F32)[:, None] * inv_freq[None, :]
    cos, sin = jnp.cos(ang), jnp.sin(ang)
    t = pos.shape[0]
    rest = HEAD_DIM - 2 * ROPE_HALF
    c = jnp.concatenate([cos, cos, jnp.ones((t, rest), F32)], axis=1)
    s_hi = jnp.concatenate([-sin, jnp.zeros((t, HEAD_DIM - ROPE_HALF), F32)], axis=1)
    s_lo = jnp.concatenate([jnp.zeros((t, ROPE_HALF), F32), sin, jnp.zeros((t, rest), F32)], axis=1)
    rep = LANES // HEAD_DIM
    return jnp.tile(c, (1, rep)), jnp.tile(s_hi, (1, rep)), jnp.tile(s_lo, (1, rep))


def _rope(y, c, s_hi, s_lo):
    w = y.shape[1]
    rep = w // LANES
    if rep > 1:
        c = jnp.concatenate([c] * rep, axis=1)
        s_hi = jnp.concatenate([s_hi] * rep, axis=1)
        s_lo = jnp.concatenate([s_lo] * rep, axis=1)
    return y * c + pltpu.roll(y, w - ROPE_HALF, axis=1) * s_hi + pltpu.roll(y, ROPE_HALF, axis=1) * s_lo


def _group_rms(y, grp, g, n):
    sq = y * y
    hi = sq.astype(BF16)
    lo = (sq - hi.astype(F32)).astype(BF16)
    ss = _dot(hi, grp) + _dot(lo, grp)
    return y * lax.rsqrt(ss * (1.0 / n) + EPS) * g


def _proj_kernel(*refs, prompt):
    (x_ref, nm_ref, wq_ref, wk_ref, wv_ref, wqi_ref, wki_ref, wwi_ref, wu_ref, wqm_ref, wg_ref,
     qn_ref, kn_ref, ikn_ref, mqn_ref, grp_ref, c_ref, shi_ref, slo_ref) = refs[:19]
    refs = refs[19:]
    if prompt:
        mk_ref, mv_ref = refs[:2]
        (k_out, v_out, ki_out, wi_out, u_out, gate_out,
         qx_out, qix_out, kbf_out, vbf_out, kid_out, memo_out) = refs[2:]
    else:
        k_out, v_out, ki_out, wi_out, u_out, gate_out, q_out, qi_out, qm_out = refs

    xn = _rms(x_ref[...], nm_ref[...], D_MODEL).astype(BF16)
    c, s_hi, s_lo = c_ref[...], shi_ref[...], slo_ref[...]
    grp = grp_ref[...]

    q = _rope(_group_rms(_dot(xn, wq_ref[...]), grp, qn_ref[...], HEAD_DIM), c, s_hi, s_lo)
    q = q * (HEAD_DIM ** -0.5)
    k = _rope(_group_rms(_dot(xn, wk_ref[...]), grp, kn_ref[...], HEAD_DIM), c, s_hi, s_lo)
    v = _dot(xn, wv_ref[...])
    qi = _rope(_dot(xn, wqi_ref[...]), c, s_hi, s_lo)
    ki2 = _dot(xn, wki_ref[...])
    ki2 = _rope(_rms(ki2, ikn_ref[...], 2 * IDX_DIM), c, s_hi, s_lo)
    wi = _dot(xn, wwi_ref[...]) * (IDX_HEADS ** -0.5 * IDX_DIM ** -0.5)
    qm = _dot(xn, wqm_ref[...])
    qm = jnp.concatenate(
        [_rms(qm[:, h * MEM_HD:(h + 1) * MEM_HD], mqn_ref[...], MEM_HD) for h in range(MEM_HEADS)], axis=1)

    k_out[...] = k
    v_out[...] = v
    ki_out[...] = ki2[:, :IDX_DIM]
    wi_out[...] = wi
    u_out[...] = _dot(xn, wu_ref[...])
    gate_out[...] = jax.nn.sigmoid(_dot(xn, wg_ref[...]))

    if not prompt:
        q_out[...] = q
        qi_out[...] = qi
        qm_out[...] = qm
        return

    kbf_out[...] = k.astype(BF16)
    vbf_out[...] = v.astype(BF16)
    kid_out[...] = ki2.astype(BF16)
    lane = lax.broadcasted_iota(I32, (q.shape[0], LANES), 1)
    for h in range(ATT_HEADS):
        pair = slice((h // 2) * LANES, (h // 2 + 1) * LANES)
        own = (lane < HEAD_DIM) if h % 2 == 0 else (lane >= HEAD_DIM)
        qx_out[h] = jnp.where(own, q[:, pair], 0.0).astype(BF16)
        qix_out[h] = jnp.where(own, qi[:, pair], 0.0).astype(BF16)

    for h in range(MEM_HEADS):
        sl = slice(h * MEM_HD, (h + 1) * MEM_HD)
        s = _dot_nt(qm[:, sl].astype(BF16), mk_ref[:, sl]) * (MEM_HD ** -0.5)
        p = jnp.exp(s - jnp.max(s, axis=-1, keepdims=True))
        p = p / jnp.sum(p, axis=-1, keepdims=True)
        memo_out[:, sl] = _dot(p.astype(BF16), mv_ref[:, sl])


def _split_w_in(w_in):
    splits = np.cumsum([ATT_W, ATT_W, ATT_W, IDX_HEADS * IDX_DIM, IDX_DIM, IDX_HEADS, SSM_W, MEM_W])
    wq, wk, wv, wqi, wki, wwi, wu, wqm, wg = jnp.split(w_in.astype(BF16), splits.tolist(), axis=1)
    wki = jnp.concatenate([wki, wki], axis=1)
    wwi = jnp.pad(wwi, ((0, 0), (0, LANES - IDX_HEADS)))
    return wq, wk, wv, wqi, wki, wwi, wu, wqm, wg


def project(x, pos, norm_mix, w_parts, q_norm, k_norm, idx_k_norm, mq_norm, mem=None, tile=256):
    t = x.shape[0]
    tile = min(tile, t)
    prompt = mem is not None
    heads_per_slab = ATT_W // HEAD_DIM
    grp = jnp.kron(jnp.eye(heads_per_slab, dtype=F32), jnp.ones((HEAD_DIM, HEAD_DIM), F32)).astype(BF16)
    c, s_hi, s_lo = _rope_tables(pos)
    tile_rep = lambda g, n: jnp.tile(g.reshape(1, -1), (1, n))
    ins = [x, norm_mix.reshape(1, -1), *w_parts,
           tile_rep(q_norm, ATT_HEADS), tile_rep(k_norm, ATT_HEADS), tile_rep(idx_k_norm, 2),
           mq_norm.reshape(1, -1), grp, c, s_hi, s_lo]
    row = lambda w: pl.BlockSpec((tile, w), lambda i: (i, 0))
    in_specs = [row(D_MODEL)] + [_full(a.shape) for a in ins[1:16]] + [row(LANES)] * 3
    outs = [(ATT_W, F32), (ATT_W, F32), (IDX_DIM, F32), (LANES, F32), (SSM_W, F32), (N_BRANCH * D_MODEL, F32)]
    out_shape = [jax.ShapeDtypeStruct((t, w), d) for w, d in outs]
    out_specs = [row(w) for w, _ in outs]
    if prompt:
        mk, mv = mem
        ins += [mk.astype(BF16), mv.astype(BF16)]
        in_specs += [_full(mk.shape), _full(mv.shape)]
        ext = pl.BlockSpec((ATT_HEADS, tile, LANES), lambda i: (0, i, 0))
        out_shape += [jax.ShapeDtypeStruct((ATT_HEADS, t, LANES), BF16)] * 2
        out_specs += [ext, ext]
        for w, d in [(ATT_W, BF16), (ATT_W, BF16), (LANES, BF16), (MEM_W, F32)]:
            out_shape.append(jax.ShapeDtypeStruct((t, w), d))
            out_specs.append(row(w))
    else:
        for w in (ATT_W, IDX_HEADS * IDX_DIM, MEM_W):
            out_shape.append(jax.ShapeDtypeStruct((t, w), F32))
            out_specs.append(row(w))
    return pl.pallas_call(
        functools.partial(_proj_kernel, prompt=prompt),
        grid=(t // tile,),
        in_specs=in_specs, out_specs=out_specs, out_shape=out_shape,
        compiler_params=_params("parallel"),
    )(*ins)


def _sortable(x):
    b = pltpu.bitcast(jnp.where(x == 0.0, 0.0, x), I32)
    return jnp.where(b < 0, b ^ 0x7FFFFFFF, b)


def _kth_largest_key(count_ge, rows, k):
    ans = jnp.where(count_ge(jnp.zeros((rows, 1), I32)) >= k, 0, INT_MIN).astype(I32)

    def body(t, ans):
        cand = ans | jnp.left_shift(jnp.int32(1), 30 - t)
        return jnp.where(count_ge(cand) >= k, cand, ans)

    return lax.fori_loop(0, 31, body, ans)


def _attn_prompt_kernel(qi_ref, q_ref, wi_ref, kid_ref, k_ref, v_ref, o_ref,
                        s_ref, m_ref, l_ref, acc_ref, *, tq, tk, topk):
    q0 = pl.program_id(0) * tq
    nkc = (q0 + tq + tk - 1) // tk
    row = q0 + lax.broadcasted_iota(I32, (tq, tk), 0)
    col0 = lax.broadcasted_iota(I32, (tq, tk), 1)
    wi = wi_ref[...]

    def score_chunk(kc, carry):
        off = pl.multiple_of(kc * tk, tk)
        kd = kid_ref[pl.ds(off, tk), :]
        acc = jnp.zeros((tq, tk), F32)
        for h in range(IDX_HEADS):
            acc = acc + wi[:, h:h + 1] * jnp.maximum(_dot_nt(qi_ref[h], kd), 0.0)
        s_ref[kc] = jnp.where(col0 + off <= row, _sortable(acc), INT_MIN)
        return carry

    lax.fori_loop(0, nkc, score_chunk, 0)

    def count_ge(cand):
        candb = jnp.broadcast_to(cand, (tq, LANES))

        def body(kc, cnt):
            blk = s_ref[kc]
            for j in range(tk // LANES):
                cnt = cnt + jnp.where(blk[:, j * LANES:(j + 1) * LANES] >= candb, 1.0, 0.0)
            return cnt

        cnt = lax.fori_loop(0, nkc, body, jnp.zeros((tq, LANES), F32))
        return jnp.sum(cnt, axis=1, keepdims=True)

    thr = jnp.maximum(_kth_largest_key(count_ge, tq, float(topk)), INT_MIN + 1)

    m_ref[...] = jnp.full(m_ref.shape, RUNNING_MAX_INIT, F32)
    l_ref[...] = jnp.zeros(l_ref.shape, F32)
    acc_ref[...] = jnp.zeros(acc_ref.shape, F32)

    def attend_chunk(kc, carry):
        off = pl.multiple_of(kc * tk, tk)
        mask = s_ref[kc] >= thr
        for h in range(ATT_HEADS):
            pair = slice((h // 2) * LANES, (h // 2 + 1) * LANES)
            s = _dot_nt(q_ref[h], k_ref[pl.ds(off, tk), pair])
            s = jnp.where(mask, s, MASKED_SCORE)
            m_old = m_ref[h]
            m_new = jnp.maximum(m_old, jnp.max(s, axis=1, keepdims=True))
            p = jnp.exp(s - m_new[:, :1])
            alpha = jnp.exp(m_old - m_new)
            l_ref[h] = alpha * l_ref[h] + jnp.sum(p, axis=1, keepdims=True)
            acc_ref[h] = alpha * acc_ref[h] + _dot(p.astype(BF16), v_ref[pl.ds(off, tk), pair])
            m_ref[h] = m_new
        return carry

    lax.fori_loop(0, nkc, attend_chunk, 0)

    lane = lax.broadcasted_iota(I32, (tq, LANES), 1)
    for j in range(ATT_HEADS // 2):
        even = acc_ref[2 * j] / l_ref[2 * j]
        odd = acc_ref[2 * j + 1] / l_ref[2 * j + 1]
        o_ref[:, j * LANES:(j + 1) * LANES] = jnp.where(lane < HEAD_DIM, even, odd)


def attn_prompt(qi_ext, q_ext, wi, ki_dup, k_bf, v_bf, tq=128, tk=512):
    t = k_bf.shape[0]
    tq, tk = min(tq, t), min(tk, t)
    topk = min(IDX_TOPK_MAX, t // 4)
    ext = pl.BlockSpec((ATT_HEADS, tq, LANES), lambda i: (0, i, 0))
    return pl.pallas_call(
        functools.partial(_attn_prompt_kernel, tq=tq, tk=tk, topk=topk),
        grid=(t // tq,),
        in_specs=[ext, ext, pl.BlockSpec((tq, LANES), lambda i: (i, 0)),
                  _full(ki_dup.shape), _full(k_bf.shape), _full(v_bf.shape)],
        out_specs=pl.BlockSpec((tq, ATT_W), lambda i: (i, 0)),
        out_shape=jax.ShapeDtypeStruct((t, ATT_W), F32),
        scratch_shapes=[pltpu.VMEM((t // tk, tq, tk), I32),
                        pltpu.VMEM((ATT_HEADS, tq, LANES), F32),
                        pltpu.VMEM((ATT_HEADS, tq, LANES), F32),
                        pltpu.VMEM((ATT_HEADS, tq, LANES), F32)],
        compiler_params=_params("arbitrary"),
    )(qi_ext, q_ext, wi, ki_dup, k_bf, v_bf)


SSM_SEG = 32
SSM_CHUNK = SUBLANES * SSM_SEG


def _cexp(re, im):
    mag = jnp.exp(re)
    return mag * jnp.cos(im), mag * jnp.sin(im)


def _ssm_prep_kernel(are_ref, aim_ref, ldt_ref, are16_ref, aim16_ref, bre_ref, bim_ref,
                     bbre_ref, bbim_ref, pre_ref, pim_ref):
    dt = jnp.exp(ldt_ref[...])
    ar, ai = are16_ref[...], aim16_ref[...]
    er, ei = _cexp(ar * dt, ai * dt)
    nr, ni = er - 1.0, ei
    den = ar * ar + ai * ai
    cr, ci = (nr * ar + ni * ai) / den, (ni * ar - nr * ai) / den
    br, bi = bre_ref[...], bim_ref[...]
    bbre_ref[...] = cr * br - ci * bi
    bbim_ref[...] = cr * bi + ci * br
    ar, ai = are_ref[...] * dt, aim_ref[...] * dt
    for j in range(SSM_SEG):
        pr, pi = _cexp(ar * (j + 1.0), ai * (j + 1.0))
        pre_ref[j] = pr
        pim_ref[j] = pi


def ssm_tables(a_re, a_im, b_re, b_im, c_re, c_im, log_dt):
    g, p, c = SSM_GROUPS, SSM_STATE, SSM_GROUP
    rep = lambda a: jnp.repeat(a, c, axis=1)
    out_shape = ([jax.ShapeDtypeStruct((g, p * c), F32)] * 2
                 + [jax.ShapeDtypeStruct((SSM_SEG, g, p), F32)] * 2)
    bbre, bbim, pre, pim = pl.pallas_call(
        _ssm_prep_kernel, out_shape=out_shape,
        compiler_params=pltpu.CompilerParams(vmem_limit_bytes=VMEM_LIMIT),
    )(a_re, a_im, log_dt.reshape(g, 1), rep(a_re), rep(a_im), b_re.reshape(g, p * c), b_im.reshape(g, p * c))
    eye = jnp.eye(g, dtype=F32)
    blockdiag_in = lambda bb: jnp.einsum('gpc,gh->gchp', bb.reshape(g, p, c), eye).reshape(g * c, g * p)
    blockdiag_out = lambda cc: jnp.einsum('gcp,gh->gphc', cc, eye).reshape(g * p, g * c)
    bmat = jnp.concatenate([blockdiag_in(bbre), blockdiag_in(bbim)], axis=1)
    cmat = jnp.concatenate([blockdiag_out(c_re), blockdiag_out(-c_im)], axis=0)
    return bmat, cmat, pre.reshape(SSM_SEG, g * p), pim.reshape(SSM_SEG, g * p)


def _ssm_out(h_all, u, cmat_ref, d_ref, wglu_ref):
    y = _dot(h_all.astype(BF16), cmat_ref[...]) + d_ref[...] * u
    z = jax.nn.gelu(y)
    return z * jax.nn.sigmoid(_dot(z.astype(BF16), wglu_ref[...]))


def _ssm_prompt_kernel(u_ref, bmat_ref, cmat_ref, pre_ref, pim_ref, d_ref, wglu_ref,
                       o_ref, hre_ref, him_ref, h_ref, carry_ref):
    n = SSM_LANES

    @pl.when(pl.program_id(0) == 0)
    def _():
        carry_ref[...] = jnp.zeros(carry_ref.shape, F32)

    u = u_ref[...]
    bu = _dot(u.astype(BF16), bmat_ref[...])
    n_blk = n // LANES
    for b in range(2 * n_blk):
        h_ref[b] = bu[:, b * LANES:(b + 1) * LANES]

    group = 4
    for b0 in range(0, n_blk, group):
        a = [(jnp.broadcast_to(pre_ref[0:1, pl.ds(b * LANES, LANES)], (SUBLANES, LANES)),
              jnp.broadcast_to(pim_ref[0:1, pl.ds(b * LANES, LANES)], (SUBLANES, LANES)))
             for b in range(b0, b0 + group)]

        def step(j, carry):
            rows = pl.ds(j, SUBLANES, stride=SSM_SEG)
            out = []
            for g in range(group):
                (a_re, a_im), (h_re, h_im) = a[g], carry[g]
                n_re = a_re * h_re - a_im * h_im + h_ref[b0 + g, rows, :]
                n_im = a_re * h_im + a_im * h_re + h_ref[n_blk + b0 + g, rows, :]
                h_ref[b0 + g, rows, :] = n_re
                h_ref[n_blk + b0 + g, rows, :] = n_im
                out.append((n_re, n_im))
            return tuple(out)

        zero = jnp.zeros((SUBLANES, LANES), F32)
        lax.fori_loop(0, SSM_SEG, step, ((zero, zero),) * group)

    for b in range(n_blk):
        re_sl, im_sl = pl.ds(b * LANES, LANES), pl.ds(n + b * LANES, LANES)
        p_re, p_im = pre_ref[:, re_sl], pim_ref[:, re_sl]
        f_re, f_im = carry_ref[:, re_sl], carry_ref[:, im_sl]
        for s in range(SUBLANES):
            rows = pl.ds(s * SSM_SEG, SSM_SEG)
            n_re = h_ref[b, rows, :] + p_re * f_re - p_im * f_im
            n_im = h_ref[n_blk + b, rows, :] + p_re * f_im + p_im * f_re
            h_ref[b, rows, :] = n_re
            h_ref[n_blk + b, rows, :] = n_im
            f_re, f_im = n_re[SSM_SEG - 1:, :], n_im[SSM_SEG - 1:, :]
        carry_ref[:, re_sl] = f_re
        carry_ref[:, im_sl] = f_im

    h_all = jnp.concatenate([h_ref[b] for b in range(2 * n_blk)], axis=1)
    o_ref[...] = _ssm_out(h_all, u, cmat_ref, d_ref, wglu_ref)
    hre_ref[...] = carry_ref[:, :n]
    him_ref[...] = carry_ref[:, n:]


def ssm_prompt(u, tables, d, w_glu):
    t = u.shape[0]
    bmat, cmat, pre, pim = tables
    n = SSM_LANES
    ins = [u, bmat.astype(BF16), cmat.astype(BF16), pre, pim, d.reshape(1, -1), w_glu.astype(BF16)]
    return pl.pallas_call(
        _ssm_prompt_kernel,
        grid=(t // SSM_CHUNK,),
        in_specs=[pl.BlockSpec((SSM_CHUNK, SSM_W), lambda i: (i, 0))] + [_full(a.shape) for a in ins[1:]],
        out_specs=[pl.BlockSpec((SSM_CHUNK, SSM_W), lambda i: (i, 0)),
                   pl.BlockSpec((1, n), lambda i: (0, 0)), pl.BlockSpec((1, n), lambda i: (0, 0))],
        out_shape=[jax.ShapeDtypeStruct((t, SSM_W), F32),
                   jax.ShapeDtypeStruct((1, n), F32), jax.ShapeDtypeStruct((1, n), F32)],
        scratch_shapes=[pltpu.VMEM((2 * n // LANES, SSM_CHUNK, LANES), F32), pltpu.VMEM((1, 2 * n), F32)],
        compiler_params=_params("arbitrary"),
    )(*ins)


def _ssm_sample_kernel(u_ref, h0re_ref, h0im_ref, bmat_ref, cmat_ref, pre_ref, pim_ref, d_ref, wglu_ref,
                       o_ref, hre_ref, him_ref):
    n = SSM_LANES
    u = u_ref[...]
    bu = jnp.dot(u, bmat_ref[...], preferred_element_type=F32, precision=lax.Precision.HIGHEST)
    a_re, a_im = pre_ref[0:1, :], pim_ref[0:1, :]
    h0_re, h0_im = h0re_ref[...], h0im_ref[...]
    h_re = bu[:, :n] + a_re * h0_re - a_im * h0_im
    h_im = bu[:, n:] + a_re * h0_im + a_im * h0_re
    hre_ref[...] = h_re
    him_ref[...] = h_im
    o_ref[...] = _ssm_out(jnp.concatenate([h_re, h_im], axis=1), u, cmat_ref, d_ref, wglu_ref)


def ssm_sample(u, h0_re, h0_im, tables, d, w_glu):
    b = u.shape[0]
    bmat, cmat, pre, pim = tables
    n = SSM_LANES
    return pl.pallas_call(
        _ssm_sample_kernel,
        out_shape=[jax.ShapeDtypeStruct((b, SSM_W), F32),
                   jax.ShapeDtypeStruct((b, n), F32), jax.ShapeDtypeStruct((b, n), F32)],
        compiler_params=pltpu.CompilerParams(vmem_limit_bytes=VMEM_LIMIT),
    )(u, h0_re.reshape(b, n), h0_im.reshape(b, n), bmat, cmat.astype(BF16), pre, pim,
      d.reshape(1, -1), w_glu.astype(BF16))


def _merge_kernel(x_ref, gate_ref, att_ref, ssm_ref, mem_ref, wa_ref, ws_ref, wm_ref, wo_ref,
                  nf_ref, wq_ref, skt_ref, h_out, hn_out, st_out):
    g = gate_ref[...]
    mixed = (g[:, :D_MODEL] * _dot(att_ref[...].astype(BF16), wa_ref[...])
             + g[:, D_MODEL:2 * D_MODEL] * _dot(ssm_ref[...].astype(BF16), ws_ref[...])
             + g[:, 2 * D_MODEL:] * _dot(mem_ref[...].astype(BF16), wm_ref[...]))
    h = x_ref[...] + _dot(mixed.astype(BF16), wo_ref[...])
    hn = _rms(h, nf_ref[...], D_MODEL).astype(BF16)
    pq = _dot(hn, wq_ref[...]).astype(BF16)
    h_out[...] = h
    hn_out[...] = hn
    st_out[...] = _dot_nt(skt_ref[...], pq)


def _sub_key_matrix(sub_keys):
    nb = PEER_HEADS * 2
    sk = sub_keys.reshape(nb, N_KEYS, PEER_DK // 2)
    eye = jnp.eye(nb, dtype=sk.dtype)
    return jnp.einsum('bnd,bc->bncd', sk, eye).reshape(nb * N_KEYS, nb * (PEER_DK // 2))


def merge(x, gate, att_o, ssm_o, mem_o, w_br_attn, w_br_ssm, w_br_mem, w_out, norm_ffn, peer_w_q,
          sub_keys, tile=256):
    t = x.shape[0]
    ws = [w.astype(BF16) for w in (w_br_attn, w_br_ssm, w_br_mem, w_out)]
    ins = [x, gate, att_o, ssm_o, mem_o, *ws, norm_ffn.reshape(1, -1), peer_w_q.astype(BF16),
           _sub_key_matrix(sub_keys).astype(BF16)]
    row = lambda w: pl.BlockSpec((tile, w), lambda i: (i, 0))
    n_scores = PEER_HEADS * 2 * N_KEYS
    return pl.pallas_call(
        _merge_kernel,
        grid=(t // tile,),
        in_specs=[row(D_MODEL), row(N_BRANCH * D_MODEL), row(ATT_W), row(SSM_W), row(MEM_W)]
        + [_full(a.shape) for a in ins[5:]],
        out_specs=[row(D_MODEL), row(D_MODEL), pl.BlockSpec((n_scores, tile), lambda i: (0, i))],
        out_shape=[jax.ShapeDtypeStruct((t, D_MODEL), F32), jax.ShapeDtypeStruct((t, D_MODEL), BF16),
                   jax.ShapeDtypeStruct((n_scores, t), F32)],
        compiler_params=_params("parallel"),
    )(*ins)


PEER_TOKENS = 256
PEER_CHUNK = 2048
NEG_INF = float("-inf")


def _top_values(x, k):
    vals = []
    for _ in range(k):
        m = jnp.max(x, axis=0, keepdims=True)
        vals.append(m)
        x = jnp.where(x == m, NEG_INF, x)
    return vals


def _peer_route(st_ref, s1_ref, s2_ref, e2_ref, f1_ref, thr_ref):
    tokens = st_ref.shape[1]
    for h in range(PEER_HEADS):
        s1 = st_ref[pl.ds(2 * h * N_KEYS, N_KEYS), :]
        s2 = st_ref[pl.ds((2 * h + 1) * N_KEYS, N_KEYS), :]
        a = _top_values(s1, PEER_TOPK)
        b = _top_values(s2, PEER_TOPK)
        cand = [a[i] + b[j] for i in range(PEER_TOPK) for j in range(PEER_TOPK) if (i + 1) * (j + 1) <= PEER_TOPK]
        pad = -len(cand) % SUBLANES
        cand = jnp.concatenate(cand + [jnp.full((pad, tokens), NEG_INF, F32)], axis=0)
        thr = _top_values(cand, PEER_TOPK)[-1]
        top = a[0] + b[0]
        z = jnp.sum(jnp.where(cand >= thr, jnp.exp(cand - top), 0.0), axis=0, keepdims=True)
        s1_ref[h] = s1
        s2_ref[h] = s2
        e2_ref[h] = jnp.exp(s2 - b[0])
        f1_ref[h] = jnp.exp(s1 - a[0]) / z
        thr_ref[h] = jnp.broadcast_to(thr, (SUBLANES, tokens))


def _peer_kernel(hn_ref, st_ref, h_ref, u_ref, vt_ref, y_ref,
                 s1_ref, s2_ref, e2_ref, f1_ref, thr_ref, acc_ref, pt_ref):
    e = pl.program_id(1)
    tokens = hn_ref.shape[0]
    rows_per_step = PEER_CHUNK // N_KEYS

    @pl.when(e == 0)
    def _():
        _peer_route(st_ref, s1_ref, s2_ref, e2_ref, f1_ref, thr_ref)
        acc_ref[...] = jnp.zeros(acc_ref.shape, F32)

    for il in range(rows_per_step):
        i = e * rows_per_step + il
        sl = slice(il * N_KEYS, (il + 1) * N_KEYS)
        w = jnp.zeros((N_KEYS, tokens), F32)
        for h in range(PEER_HEADS):
            pair = s2_ref[h] + s1_ref[h, pl.ds(i, 1), :]
            val = e2_ref[h] * f1_ref[h, pl.ds(i, 1), :]
            w = w + jnp.where(pair >= thr_ref[h, 0:1, :], val, 0.0)
        act = jax.nn.gelu(_dot_nt(u_ref[sl, :], hn_ref[...]))
        pt_ref[sl, :] = (w * act).astype(BF16)
    acc_ref[...] += _dot(vt_ref[...], pt_ref[...])

    @pl.when(e == pl.num_programs(1) - 1)
    def _():
        y_ref[...] = h_ref[...] + acc_ref[...].T


def peer(hn, st, h, u_bf, vt_bf):
    t = hn.shape[0]
    tb = PEER_TOKENS
    n_scores = st.shape[0]
    return pl.pallas_call(
        _peer_kernel,
        grid=(t // tb, N_EXPERTS // PEER_CHUNK),
        in_specs=[pl.BlockSpec((tb, D_MODEL), lambda i, e: (i, 0)),
                  pl.BlockSpec((n_scores, tb), lambda i, e: (0, i)),
                  pl.BlockSpec((tb, D_MODEL), lambda i, e: (i, 0)),
                  pl.BlockSpec((PEER_CHUNK, D_MODEL), lambda i, e: (e, 0)),
                  pl.BlockSpec((D_MODEL, PEER_CHUNK), lambda i, e: (0, e))],
        out_specs=pl.BlockSpec((tb, D_MODEL), lambda i, e: (i, 0)),
        out_shape=jax.ShapeDtypeStruct((t, D_MODEL), F32),
        scratch_shapes=[pltpu.VMEM((PEER_HEADS, N_KEYS, tb), F32)] * 4
        + [pltpu.VMEM((PEER_HEADS, SUBLANES, tb), F32),
           pltpu.VMEM((D_MODEL, tb), F32), pltpu.VMEM((PEER_CHUNK, tb), BF16)],
        compiler_params=_params("parallel", "arbitrary"),
    )(hn, st, h, u_bf, vt_bf)


def _bf16_round(x):
    return x.astype(BF16).astype(F32)


def _attend_rows(q, kb, vb):
    s = jnp.sum(_bf16_round(kb) * _bf16_round(q)[None], axis=2, keepdims=True)
    p = jnp.exp(s - jnp.max(s, axis=0, keepdims=True))
    p = p / jnp.sum(p, axis=0, keepdims=True)
    return jnp.sum(_bf16_round(p) * _bf16_round(vb), axis=0)


def _mem_sample_kernel(qm_ref, mk_ref, mv_ref, o_ref):
    o_ref[0] = _attend_rows(qm_ref[0] * (MEM_HD ** -0.5), mk_ref[0], mv_ref[0])


def mem_attend_sample(qm, cache_mem_k, cache_mem_v):
    b = qm.shape[0]
    blk = pl.BlockSpec((1, MEM_HEADS, MEM_HD), lambda i: (i, 0, 0))
    cache = pl.BlockSpec((1, MEM_TOKENS, MEM_HEADS, MEM_HD), lambda i: (i, 0, 0, 0))
    return pl.pallas_call(
        _mem_sample_kernel, grid=(b,), in_specs=[blk, cache, cache], out_specs=blk,
        out_shape=jax.ShapeDtypeStruct((b, MEM_HEADS, MEM_HD), F32),
        compiler_params=_params("parallel"),
    )(qm.reshape(b, MEM_HEADS, MEM_HD), cache_mem_k, cache_mem_v)


IDX_KEY_CHUNK = 2048


def _idx_sample_kernel(pt_ref, qi_ref, wi_ref, kin_ref, cache_ref, o_ref, buf_ref, sem):
    b, nb = pl.program_id(0), pl.num_programs(0)
    n_pages = pt_ref.shape[1]

    def page_copy(bb, slot, j):
        dst = buf_ref.at[slot, pl.ds(pl.multiple_of(j * PAGE_SIZE, PAGE_SIZE), PAGE_SIZE), :]
        return pltpu.make_async_copy(cache_ref.at[pt_ref[bb, j]], dst, sem.at[slot])

    def start_all(bb, slot):
        lax.fori_loop(0, n_pages, lambda j, c: (page_copy(bb, slot, j).start(), c)[1], 0)

    slot = b % 2

    @pl.when(b == 0)
    def _():
        start_all(0, 0)

    @pl.when(b + 1 < nb)
    def _():
        start_all(b + 1, 1 - slot)

    lax.fori_loop(0, n_pages, lambda j, c: (page_copy(b, slot, j).wait(), c)[1], 0)

    qi = qi_ref[0]
    w = wi_ref[0]
    qi_bf = qi.astype(BF16)
    for c in range(PAST_LEN // IDX_KEY_CHUNK):
        keys = buf_ref[slot, pl.ds(c * IDX_KEY_CHUNK, IDX_KEY_CHUNK), :].astype(BF16)
        s = jnp.maximum(_dot_nt(qi_bf, keys), 0.0)
        o_ref[0, :, pl.ds(c * IDX_KEY_CHUNK, IDX_KEY_CHUNK)] = jnp.sum(w * s, axis=0, keepdims=True)
    s_new = jnp.sum(_bf16_round(qi) * _bf16_round(kin_ref[0]), axis=1, keepdims=True)
    s_new = jnp.sum(w * jnp.maximum(s_new, 0.0), axis=0, keepdims=True)
    lane = lax.broadcasted_iota(I32, (1, LANES), 1)
    o_ref[0, :, pl.ds(PAST_LEN, LANES)] = jnp.where(lane == 0, s_new, NEG_INF)


def idx_scores_sample(qi, wi, ki_new, cache_idx_k, page_table):
    b = qi.shape[0]
    width = PAST_LEN + LANES
    grid_spec = pltpu.PrefetchScalarGridSpec(
        num_scalar_prefetch=1, grid=(b,),
        in_specs=[pl.BlockSpec((1, IDX_HEADS, IDX_DIM), lambda i, pt: (i, 0, 0)),
                  pl.BlockSpec((1, IDX_HEADS, 1), lambda i, pt: (i, 0, 0)),
                  pl.BlockSpec((1, 1, IDX_DIM), lambda i, pt: (i, 0, 0)),
                  pl.BlockSpec(memory_space=pl.ANY)],
        out_specs=pl.BlockSpec((1, 1, width), lambda i, pt: (i, 0, 0)),
        scratch_shapes=[pltpu.VMEM((2, PAST_LEN, IDX_DIM), F32), pltpu.SemaphoreType.DMA((2,))])
    out = pl.pallas_call(
        _idx_sample_kernel, grid_spec=grid_spec,
        out_shape=jax.ShapeDtypeStruct((b, 1, width), F32),
        compiler_params=_params("arbitrary"),
    )(page_table, qi.reshape(b, IDX_HEADS, IDX_DIM), wi[:, :IDX_HEADS].reshape(b, IDX_HEADS, 1),
      ki_new.reshape(b, 1, IDX_DIM), cache_idx_k)
    return out.reshape(b, width)


def _select_kernel(sc_ref, sel_ref, key_ref, pc_ref, *, topk, n_valid):
    b = sc_ref.shape[0]
    n_blk = key_ref.shape[0]
    lane = lax.broadcasted_iota(I32, (b, LANES), 1)
    for j in range(n_blk):
        key = _sortable(sc_ref[:, j * LANES:(j + 1) * LANES])
        key_ref[j] = jnp.where(lane + j * LANES < n_valid, key, INT_MIN)

    def count_ge(cand):
        candb = jnp.broadcast_to(cand, (b, LANES))
        cnt = lax.fori_loop(0, n_blk, lambda j, c: c + jnp.where(key_ref[j] >= candb, 1.0, 0.0),
                            jnp.zeros((b, LANES), F32))
        return jnp.sum(cnt, axis=1, keepdims=True)

    thr = jnp.maximum(_kth_largest_key(count_ge, b, float(topk)), INT_MIN + 1)
    need = float(topk) - count_ge(thr + 1)
    ri = lax.broadcasted_iota(I32, (LANES, LANES), 0)
    ci = lax.broadcasted_iota(I32, (LANES, LANES), 1)
    tri = jnp.where(ri <= ci, 1.0, 0.0).astype(BF16)

    def prefix(j, carry):
        run_tie, run_sel = carry
        key = key_ref[j]
        tie = key == thr
        tie_f = jnp.where(tie, 1.0, 0.0)
        tie_rank = run_tie + _dot(tie_f.astype(BF16), tri)
        sel = jnp.where(key > thr, 1.0, jnp.where(tie & (tie_rank <= need), 1.0, 0.0))
        pc = run_sel + _dot(sel.astype(BF16), tri)
        pc_ref[j] = pc
        return run_tie + jnp.sum(tie_f, axis=1, keepdims=True), run_sel + jnp.sum(sel, axis=1, keepdims=True)

    zero = jnp.zeros((b, 1), F32)
    lax.fori_loop(0, n_blk, prefix, (zero, zero))

    lane_k = lax.broadcasted_iota(I32, (b, topk), 1)

    def place(k, out):
        kf = jnp.asarray(k, F32)
        cnt = lax.fori_loop(0, n_blk, lambda j, c: c + jnp.where(pc_ref[j] <= kf, 1.0, 0.0),
                            jnp.zeros((b, LANES), F32))
        return jnp.where(lane_k == k, jnp.sum(cnt, axis=1, keepdims=True).astype(I32), out)

    sel_ref[...] = lax.fori_loop(0, topk, place, jnp.zeros((b, topk), I32))


def select_topk(scores, topk, n_valid):
    b, width = scores.shape
    n_blk = width // LANES
    return pl.pallas_call(
        functools.partial(_select_kernel, topk=topk, n_valid=n_valid),
        out_shape=jax.ShapeDtypeStruct((b, topk), I32),
        scratch_shapes=[pltpu.VMEM((n_blk, b, LANES), I32), pltpu.VMEM((n_blk, b, LANES), F32)],
        compiler_params=pltpu.CompilerParams(vmem_limit_bytes=VMEM_LIMIT),
    )(scores)


def _attn_sample_kernel(sel_ref, pt_ref, q_ref, kn_ref, vn_ref, ck_ref, cv_ref, o_ref, kbuf, vbuf, sem):
    b, nb = pl.program_id(0), pl.num_programs(0)
    topk = sel_ref.shape[1]

    def row_copies(bb, slot, j, new):
        if new:
            src_k, src_v = kn_ref.at[bb, 0], vn_ref.at[bb, 0]
        else:
            pos = sel_ref[bb, j]
            page = pt_ref[bb, pos // PAGE_SIZE]
            src_k, src_v = ck_ref.at[page, pos % PAGE_SIZE], cv_ref.at[page, pos % PAGE_SIZE]
        return (pltpu.make_async_copy(src_k, kbuf.at[slot, j], sem.at[0, slot]),
                pltpu.make_async_copy(src_v, vbuf.at[slot, j], sem.at[1, slot]))

    def start_all(bb, slot):
        def body(j, c):
            is_new = sel_ref[bb, j] >= PAST_LEN

            @pl.when(is_new)
            def _():
                for cp in row_copies(bb, slot, j, True):
                    cp.start()

            @pl.when(jnp.logical_not(is_new))
            def _():
                for cp in row_copies(bb, slot, j, False):
                    cp.start()

            return c

        lax.fori_loop(0, topk, body, 0)

    slot = b % 2

    @pl.when(b == 0)
    def _():
        start_all(0, 0)

    @pl.when(b + 1 < nb)
    def _():
        start_all(b + 1, 1 - slot)

    def wait_row(j, c):
        for cp in row_copies(b, slot, j, True):
            cp.wait()
        return c

    lax.fori_loop(0, topk, wait_row, 0)
    o_ref[0] = _attend_rows(q_ref[0], kbuf[slot], vbuf[slot])


def attn_sample(sel, page_table, q, k_new, v_new, cache_k, cache_v):
    b, topk = sel.shape
    shape4 = (b, 1, ATT_HEADS, HEAD_DIM)
    blk = pl.BlockSpec((1, ATT_HEADS, HEAD_DIM), lambda i, *_: (i, 0, 0))
    any_spec = pl.BlockSpec(memory_space=pl.ANY)
    grid_spec = pltpu.PrefetchScalarGridSpec(
        num_scalar_prefetch=2, grid=(b,),
        in_specs=[blk, any_spec, any_spec, any_spec, any_spec],
        out_specs=blk,
        scratch_shapes=[pltpu.VMEM((2, topk, ATT_HEADS, HEAD_DIM), F32),
                        pltpu.VMEM((2, topk, ATT_HEADS, HEAD_DIM), F32),
                        pltpu.SemaphoreType.DMA((2, 2))])
    out = pl.pallas_call(
        _attn_sample_kernel, grid_spec=grid_spec,
        out_shape=jax.ShapeDtypeStruct((b, ATT_HEADS, HEAD_DIM), F32),
        compiler_params=_params("arbitrary"),
    )(sel, page_table, q.reshape(b, ATT_HEADS, HEAD_DIM), k_new.reshape(shape4), v_new.reshape(shape4),
      cache_k, cache_v)
    return out.reshape(b, ATT_W)


def _pad_rows(a, rows):
    return jnp.pad(a, ((0, rows - a.shape[0]), (0, 0)))


def kernel(x_prompt, x_sample, cache_k, cache_v, cache_idx_k, cache_mem_k, cache_mem_v, state_ssm_re, state_ssm_im, page_table, mem_prompt, norm_mix, w_in, q_norm, k_norm, idx_k_norm, mq_norm, ssm_a_re, ssm_a_im, ssm_b_re, ssm_b_im, ssm_c_re, ssm_c_im, ssm_d, ssm_log_dt, w_glu, mem_norm, w_mem_kv, mk_norm, w_br_attn, w_br_ssm, w_br_mem, w_out, norm_ffn, peer_w_q, peer_sub_keys, peer_u, peer_v):
    seq = x_prompt.shape[1]
    bd = x_sample.shape[0]
    w_parts = _split_w_in(w_in)
    tables = ssm_tables(ssm_a_re, ssm_a_im, ssm_b_re, ssm_b_im, ssm_c_re, ssm_c_im, ssm_log_dt)
    u_bf = peer_u.astype(BF16)
    vt_bf = peer_v.T.astype(BF16)
    tail = (w_br_attn, w_br_ssm, w_br_mem, w_out, norm_ffn, peer_w_q, peer_sub_keys)
    norms = (q_norm, k_norm, idx_k_norm, mq_norm)

    mk, mv = mem_kv(mem_prompt[0], mem_norm, w_mem_kv, mk_norm)
    (k_p, v_p, ki_p, wi, u, gate, q_ext, qi_ext, k_bf, v_bf, ki_dup, mem_o) = project(
        x_prompt[0], jnp.arange(seq), norm_mix, w_parts, *norms, mem=(mk, mv))
    att_o = attn_prompt(qi_ext, q_ext, wi, ki_dup, k_bf, v_bf)
    ssm_o, hre_p, him_p = ssm_prompt(u, tables, ssm_d, w_glu)
    h, hn, st = merge(x_prompt[0], gate, att_o, ssm_o, mem_o, *tail)
    y_prompt = peer(hn, st, h, u_bf, vt_bf)

    (k_s, v_s, ki_s, wi, u, gate, q, qi, qm) = project(
        x_sample[:, 0], jnp.full((bd,), PAST_LEN), norm_mix, w_parts, *norms)
    scores = idx_scores_sample(qi, wi, ki_s, cache_idx_k, page_table)
    sel = select_topk(scores, min(IDX_TOPK_MAX, (PAST_LEN + 1) // 4), PAST_LEN + 1)
    att_o = attn_sample(sel, page_table, q, k_s, v_s, cache_k, cache_v)
    ssm_o, hre_s, him_s = ssm_sample(u, state_ssm_re, state_ssm_im, tables, ssm_d, w_glu)
    mem_o = mem_attend_sample(qm, cache_mem_k, cache_mem_v).reshape(bd, MEM_W)
    pad = lambda a: _pad_rows(a, PEER_TOKENS)
    h, hn, st = merge(pad(x_sample[:, 0]), pad(gate), pad(att_o), pad(ssm_o), pad(mem_o), *tail)
    y_sample = peer(hn, st, h, u_bf, vt_bf)[:bd]

    heads = lambda a: a.reshape(a.shape[0], ATT_HEADS, HEAD_DIM)
    state = lambda a, n: a.reshape(n, SSM_GROUPS, SSM_STATE)
    return (y_prompt[None], y_sample[:, None],
            heads(k_p)[None], heads(v_p)[None], ki_p[None],
            mk.reshape(1, MEM_TOKENS, MEM_HEADS, MEM_HD), mv.reshape(1, MEM_TOKENS, MEM_HEADS, MEM_HD),
            state(hre_p, 1), state(him_p, 1),
            heads(k_s)[:, None], heads(v_s)[:, None], ki_s[:, None],
            state(hre_s, bd), state(him_s, bd))
```

```python
import functools
import math

import jax
import jax.numpy as jnp
import numpy as np
from jax import lax
from jax.experimental import pallas as pl
from jax.experimental.pallas import tpu as pltpu

F32 = jnp.float32
BF16 = jnp.bfloat16
I32 = jnp.int32

D_MODEL = 1024
PAST_LEN = 16384
PAGE_SIZE = 128
ATT_HEADS = 8
HEAD_DIM = 64
ATT_W = ATT_HEADS * HEAD_DIM
IDX_HEADS = 8
IDX_DIM = 64
IDX_TOPK_MAX = 256
ROPE_THETA = 500000.0
ROPE_HALF = HEAD_DIM // 4 // 2
SSM_GROUP = 16
SSM_W = 512
SSM_GROUPS = SSM_W // SSM_GROUP
SSM_STATE = 64
SSM_LANES = SSM_GROUPS * SSM_STATE
MEM_TOKENS = 256
MEM_HEADS = 4
MEM_HD = 128
MEM_W = MEM_HEADS * MEM_HD
N_BRANCH = 3
PEER_HEADS = 8
N_KEYS = 128
N_EXPERTS = N_KEYS * N_KEYS
PEER_DK = 128
PEER_TOPK = 16
EPS = 1e-6

LANES = 128
SUBLANES = 8
VMEM_LIMIT = 56 * 1024 * 1024
INT_MIN = -(2 ** 31)
MASKED_SCORE = -1e30
RUNNING_MAX_INIT = -5e29

NT_DIMS = (((1,), (1,)), ((), ()))


def _params(*sem):
    return pltpu.CompilerParams(dimension_semantics=sem, vmem_limit_bytes=VMEM_LIMIT)


def _full(shape):
    nd = len(shape)
    return pl.BlockSpec(shape, lambda *_: (0,) * nd, pipeline_mode=pl.Buffered(1))


def _rms(x, g, n):
    ms = jnp.sum(x * x, axis=-1, keepdims=True) * (1.0 / n)
    return x * lax.rsqrt(ms + EPS) * g


def _dot(a, b):
    return jnp.dot(a, b, preferred_element_type=F32)


def _dot_nt(a, b):
    return lax.dot_general(a, b, NT_DIMS, preferred_element_type=F32)


def _mem_kv_kernel(mem_ref, mn_ref, w_ref, mkn_ref, mk_ref, mv_ref):
    xn = _rms(mem_ref[...], mn_ref[...], D_MODEL).astype(BF16)
    y = _dot(xn, w_ref[...])
    for h in range(MEM_HEADS):
        sl = slice(h * MEM_HD, (h + 1) * MEM_HD)
        mk_ref[:, sl] = _rms(y[:, sl], mkn_ref[...], MEM_HD)
    mv_ref[...] = y[:, MEM_W:]


def mem_kv(mem, mem_norm, w_mem_kv, mk_norm):
    m = mem.shape[0]
    return pl.pallas_call(
        _mem_kv_kernel,
        out_shape=(jax.ShapeDtypeStruct((m, MEM_W), F32), jax.ShapeDtypeStruct((m, MEM_W), F32)),
        compiler_params=pltpu.CompilerParams(vmem_limit_bytes=VMEM_LIMIT),
    )(mem, mem_norm.reshape(1, -1), w_mem_kv.astype(BF16), mk_norm.reshape(1, -1))


def _rope_tables(pos):
    inv_freq = ROPE_THETA ** (-jnp.arange(ROPE_HALF, dtype=F32) / ROPE_HALF)
    ang = pos.astype(F32)[:, None] * inv_freq[None, :]
    cos, sin = jnp.cos(ang), jnp.sin(ang)
    t = pos.shape[0]
    rest = HEAD_DIM - 2 * ROPE_HALF
    c = jnp.concatenate([cos, cos, jnp.ones((t, rest), F32)], axis=1)
    s_hi = jnp.concatenate([-sin, jnp.zeros((t, HEAD_DIM - ROPE_HALF), F32)], axis=1)
    s_lo = jnp.concatenate([jnp.zeros((t, ROPE_HALF), F32), sin, jnp.zeros((t, rest), F32)], axis=1)
    rep = LANES // HEAD_DIM
    return jnp.tile(c, (1, rep)), jnp.tile(s_hi, (1, rep)), jnp.tile(s_lo, (1, rep))


def _rope(y, c, s_hi, s_lo):
    w = y.shape[1]
    rep = w // LANES
    if rep > 1:
        c = jnp.concatenate([c] * rep, axis=1)
        s_hi = jnp.concatenate([s_hi] * rep, axis=1)
        s_lo = jnp.concatenate([s_lo] * rep, axis=1)
    return y * c + pltpu.roll(y, w - ROPE_HALF, axis=1) * s_hi + pltpu.roll(y, ROPE_HALF, axis=1) * s_lo


def _group_rms(y, grp, g, n):
    sq = y * y
    hi = sq.astype(BF16)
    lo = (sq - hi.astype(F32)).astype(BF16)
    ss = _dot(hi, grp) + _dot(lo, grp)
    return y * lax.rsqrt(ss * (1.0 / n) + EPS) * g


def _proj_kernel(*refs, prompt):
    (x_ref, nm_ref, wq_ref, wk_ref, wv_ref, wqi_ref, wki_ref, wwi_ref, wu_ref, wqm_ref, wg_ref,
     qn_ref, kn_ref, ikn_ref, mqn_ref, grp_ref, c_ref, shi_ref, slo_ref) = refs[:19]
    refs = refs[19:]
    if prompt:
        mk_ref, mv_ref = refs[:2]
        (k_out, v_out, ki_out, wi_out, u_out, gate_out,
         qx_out, qix_out, kbf_out, vbf_out, kid_out, memo_out) = refs[2:]
    else:
        k_out, v_out, ki_out, wi_out, u_out, gate_out, q_out, qi_out, qm_out = refs

    xn = _rms(x_ref[...], nm_ref[...], D_MODEL).astype(BF16)
    c, s_hi, s_lo = c_ref[...], shi_ref[...], slo_ref[...]
    grp = grp_ref[...]

    q = _rope(_group_rms(_dot(xn, wq_ref[...]), grp, qn_ref[...], HEAD_DIM), c, s_hi, s_lo)
    q = q * (HEAD_DIM ** -0.5)
    k = _rope(_group_rms(_dot(xn, wk_ref[...]), grp, kn_ref[...], HEAD_DIM), c, s_hi, s_lo)
    v = _dot(xn, wv_ref[...])
    qi = _rope(_dot(xn, wqi_ref[...]), c, s_hi, s_lo)
    ki2 = _dot(xn, wki_ref[...])
    ki2 = _rope(_rms(ki2, ikn_ref[...], 2 * IDX_DIM), c, s_hi, s_lo)
    wi = _dot(xn, wwi_ref[...]) * (IDX_HEADS ** -0.5 * IDX_DIM ** -0.5)
    qm = _dot(xn, wqm_ref[...])
    qm = jnp.concatenate(
        [_rms(qm[:, h * MEM_HD:(h + 1) * MEM_HD], mqn_ref[...], MEM_HD) for h in range(MEM_HEADS)], axis=1)

    k_out[...] = k
    v_out[...] = v
    ki_out[...] = ki2[:, :IDX_DIM]
    wi_out[...] = wi
    u_out[...] = _dot(xn, wu_ref[...])
    gate_out[...] = jax.nn.sigmoid(_dot(xn, wg_ref[...]))

    if not prompt:
        q_out[...] = q
        qi_out[...] = qi
        qm_out[...] = qm
        return

    kbf_out[...] = k.astype(BF16)
    vbf_out[...] = v.astype(BF16)
    kid_out[...] = ki2.astype(BF16)
    lane = lax.broadcasted_iota(I32, (q.shape[0], LANES), 1)
    for h in range(ATT_HEADS):
        pair = slice((h // 2) * LANES, (h // 2 + 1) * LANES)
        own = (lane < HEAD_DIM) if h % 2 == 0 else (lane >= HEAD_DIM)
        qx_out[h] = jnp.where(own, q[:, pair], 0.0).astype(BF16)
        qix_out[h] = jnp.where(own, qi[:, pair], 0.0).astype(BF16)

    for h in range(MEM_HEADS):
        sl = slice(h * MEM_HD, (h + 1) * MEM_HD)
        s = _dot_nt(qm[:, sl].astype(BF16), mk_ref[:, sl]) * (MEM_HD ** -0.5)
        p = jnp.exp(s - jnp.max(s, axis=-1, keepdims=True))
        p = p / jnp.sum(p, axis=-1, keepdims=True)
        memo_out[:, sl] = _dot(p.astype(BF16), mv_ref[:, sl])


def _split_w_in(w_in):
    splits = np.cumsum([ATT_W, ATT_W, ATT_W, IDX_HEADS * IDX_DIM, IDX_DIM, IDX_HEADS, SSM_W, MEM_W])
    wq, wk, wv, wqi, wki, wwi, wu, wqm, wg = jnp.split(w_in.astype(BF16), splits.tolist(), axis=1)
    wki = jnp.concatenate([wki, wki], axis=1)
    wwi = jnp.pad(wwi, ((0, 0), (0, LANES - IDX_HEADS)))
    return wq, wk, wv, wqi, wki, wwi, wu, wqm, wg


def project(x, pos, norm_mix, w_parts, q_norm, k_norm, idx_k_norm, mq_norm, mem=None, tile=256):
    t = x.shape[0]
    tile = min(tile, t)
    prompt = mem is not None
    heads_per_slab = ATT_W // HEAD_DIM
    grp = jnp.kron(jnp.eye(heads_per_slab, dtype=F32), jnp.ones((HEAD_DIM, HEAD_DIM), F32)).astype(BF16)
    c, s_hi, s_lo = _rope_tables(pos)
    tile_rep = lambda g, n: jnp.tile(g.reshape(1, -1), (1, n))
    ins = [x, norm_mix.reshape(1, -1), *w_parts,
           tile_rep(q_norm, ATT_HEADS), tile_rep(k_norm, ATT_HEADS), tile_rep(idx_k_norm, 2),
           mq_norm.reshape(1, -1), grp, c, s_hi, s_lo]
    row = lambda w: pl.BlockSpec((tile, w), lambda i: (i, 0))
    in_specs = [row(D_MODEL)] + [_full(a.shape) for a in ins[1:16]] + [row(LANES)] * 3
    outs = [(ATT_W, F32), (ATT_W, F32), (IDX_DIM, F32), (LANES, F32), (SSM_W, F32), (N_BRANCH * D_MODEL, F32)]
    out_shape = [jax.ShapeDtypeStruct((t, w), d) for w, d in outs]
    out_specs = [row(w) for w, _ in outs]
    if prompt:
        mk, mv = mem
        ins += [mk.astype(BF16), mv.astype(BF16)]
        in_specs += [_full(mk.shape), _full(mv.shape)]
        ext = pl.BlockSpec((ATT_HEADS, tile, LANES), lambda i: (0, i, 0))
        out_shape += [jax.ShapeDtypeStruct((ATT_HEADS, t, LANES), BF16)] * 2
        out_specs += [ext, ext]
        for w, d in [(ATT_W, BF16), (ATT_W, BF16), (LANES, BF16), (MEM_W, F32)]:
            out_shape.append(jax.ShapeDtypeStruct((t, w), d))
            out_specs.append(row(w))
    else:
        for w in (ATT_W, IDX_HEADS * IDX_DIM, MEM_W):
            out_shape.append(jax.ShapeDtypeStruct((t, w), F32))
            out_specs.append(row(w))
    return pl.pallas_call(
        functools.partial(_proj_kernel, prompt=prompt),
        grid=(t // tile,),
        in_specs=in_specs, out_specs=out_specs, out_shape=out_shape,
        compiler_params=_params("parallel"),
    )(*ins)


def _sortable(x):
    b = pltpu.bitcast(jnp.where(x == 0.0, 0.0, x), I32)
    return jnp.where(b < 0, b ^ 0x7FFFFFFF, b)


def _kth_largest_key(count_ge, rows, k):
    ans = jnp.where(count_ge(jnp.zeros((rows, 1), I32)) >= k, 0, INT_MIN).astype(I32)

    def body(t, ans):
        cand = ans | jnp.left_shift(jnp.int32(1), 30 - t)
        return jnp.where(count_ge(cand) >= k, cand, ans)

    return lax.fori_loop(0, 31, body, ans)


def _attn_prompt_kernel(qi_ref, q_ref, wi_ref, kid_ref, k_ref, v_ref, o_ref,
                        s_ref, m_ref, l_ref, acc_ref, bias_ref, qk_ref, p_ref, *, tq, tk, topk):
    q0 = pl.program_id(0) * tq
    nkc = (q0 + tq + tk - 1) // tk
    row = q0 + lax.broadcasted_iota(I32, (tq, tk), 0)
    col0 = lax.broadcasted_iota(I32, (tq, tk), 1)
    wi = wi_ref[...]

    def score_chunk(kc, carry):
        off = pl.multiple_of(kc * tk, tk)
        kd = kid_ref[pl.ds(off, tk), :]
        acc = jnp.zeros((tq, tk), F32)
        for h in range(IDX_HEADS):
            acc = acc + wi[:, h:h + 1] * jnp.maximum(_dot_nt(qi_ref[h], kd), 0.0)
        s_ref[kc] = jnp.where(col0 + off <= row, _sortable(acc), INT_MIN)
        return carry

    lax.fori_loop(0, nkc, score_chunk, 0)

    def count_ge(cand):
        candb = jnp.broadcast_to(cand, (tq, LANES))

        def body(kc, cnt):
            blk = s_ref[kc]
            for j in range(tk // LANES):
                cnt = cnt + jnp.where(blk[:, j * LANES:(j + 1) * LANES] >= candb, 1.0, 0.0)
            return cnt

        cnt = lax.fori_loop(0, nkc, body, jnp.zeros((tq, LANES), F32))
        return jnp.sum(cnt, axis=1, keepdims=True)

    thr = jnp.maximum(_kth_largest_key(count_ge, tq, float(topk)), INT_MIN + 1)

    m_ref[...] = jnp.full(m_ref.shape, RUNNING_MAX_INIT, F32)
    l_ref[...] = jnp.zeros(l_ref.shape, F32)
    acc_ref[...] = jnp.zeros(acc_ref.shape, F32)

    cols = [slice(j * LANES, (j + 1) * LANES) for j in range(tk // LANES)]

    def attend_chunk(kc, carry):
        off = pl.multiple_of(kc * tk, tk)
        bias_ref[...] = jnp.where(s_ref[kc] >= thr, 0.0, MASKED_SCORE)
        for h in range(ATT_HEADS):
            pair = slice((h // 2) * LANES, (h // 2 + 1) * LANES)
            qk_ref[h] = _dot_nt(q_ref[h], k_ref[pl.ds(off, tk), pair])
        for h in range(ATT_HEADS):
            mx = functools.reduce(jnp.maximum, [qk_ref[h, :, c] + bias_ref[:, c] for c in cols])
            m_old = m_ref[h]
            m_new = jnp.maximum(m_old, jnp.max(mx, axis=1, keepdims=True))
            alpha = jnp.exp(m_old - m_new)
            psum = jnp.zeros((tq, LANES), F32)
            for c in cols:
                p = jnp.exp(qk_ref[h, :, c] + bias_ref[:, c] - m_new)
                psum = psum + p
                p_ref[h, :, c] = p.astype(BF16)
            m_ref[h] = m_new
            l_ref[h] = alpha * l_ref[h] + psum
            acc_ref[h] = alpha * acc_ref[h]
        for h in range(ATT_HEADS):
            pair = slice((h // 2) * LANES, (h // 2 + 1) * LANES)
            acc_ref[h] += _dot(p_ref[h], v_ref[pl.ds(off, tk), pair])
        return carry

    lax.fori_loop(0, nkc, attend_chunk, 0)

    lane = lax.broadcasted_iota(I32, (tq, LANES), 1)
    for j in range(ATT_HEADS // 2):
        even = acc_ref[2 * j] / jnp.sum(l_ref[2 * j], axis=1, keepdims=True)
        odd = acc_ref[2 * j + 1] / jnp.sum(l_ref[2 * j + 1], axis=1, keepdims=True)
        o_ref[:, j * LANES:(j + 1) * LANES] = jnp.where(lane < HEAD_DIM, even, odd)


def attn_prompt(qi_ext, q_ext, wi, ki_dup, k_bf, v_bf, tq=128, tk=512):
    t = k_bf.shape[0]
    tq, tk = min(tq, t), min(tk, t)
    topk = min(IDX_TOPK_MAX, t // 4)
    ext = pl.BlockSpec((ATT_HEADS, tq, LANES), lambda i: (0, i, 0))
    return pl.pallas_call(
        functools.partial(_attn_prompt_kernel, tq=tq, tk=tk, topk=topk),
        grid=(t // tq,),
        in_specs=[ext, ext, pl.BlockSpec((tq, LANES), lambda i: (i, 0)),
                  _full(ki_dup.shape), _full(k_bf.shape), _full(v_bf.shape)],
        out_specs=pl.BlockSpec((tq, ATT_W), lambda i: (i, 0)),
        out_shape=jax.ShapeDtypeStruct((t, ATT_W), F32),
        scratch_shapes=[pltpu.VMEM((t // tk, tq, tk), I32),
                        pltpu.VMEM((ATT_HEADS, tq, LANES), F32),
                        pltpu.VMEM((ATT_HEADS, tq, LANES), F32),
                        pltpu.VMEM((ATT_HEADS, tq, LANES), F32),
                        pltpu.VMEM((tq, tk), F32),
                        pltpu.VMEM((ATT_HEADS, tq, tk), F32),
                        pltpu.VMEM((ATT_HEADS, tq, tk), BF16)],
        compiler_params=_params("arbitrary"),
    )(qi_ext, q_ext, wi, ki_dup, k_bf, v_bf)


SSM_SEG = 32
SSM_CHUNK = SUBLANES * SSM_SEG


def _cexp(re, im):
    mag = jnp.exp(re)
    return mag * jnp.cos(im), mag * jnp.sin(im)


def _ssm_prep_kernel(are_ref, aim_ref, ldt_ref, are16_ref, aim16_ref, bre_ref, bim_ref,
                     bbre_ref, bbim_ref, pre_ref, pim_ref):
    dt = jnp.exp(ldt_ref[...])
    ar, ai = are16_ref[...], aim16_ref[...]
    er, ei = _cexp(ar * dt, ai * dt)
    nr, ni = er - 1.0, ei
    den = ar * ar + ai * ai
    cr, ci = (nr * ar + ni * ai) / den, (ni * ar - nr * ai) / den
    br, bi = bre_ref[...], bim_ref[...]
    bbre_ref[...] = cr * br - ci * bi
    bbim_ref[...] = cr * bi + ci * br
    ar, ai = are_ref[...] * dt, aim_ref[...] * dt
    for j in range(SSM_SEG):
        pr, pi = _cexp(ar * (j + 1.0), ai * (j + 1.0))
        pre_ref[j] = pr
        pim_ref[j] = pi


def ssm_tables(a_re, a_im, b_re, b_im, c_re, c_im, log_dt):
    g, p, c = SSM_GROUPS, SSM_STATE, SSM_GROUP
    rep = lambda a: jnp.repeat(a, c, axis=1)
    out_shape = ([jax.ShapeDtypeStruct((g, p * c), F32)] * 2
                 + [jax.ShapeDtypeStruct((SSM_SEG, g, p), F32)] * 2)
    bbre, bbim, pre, pim = pl.pallas_call(
        _ssm_prep_kernel, out_shape=out_shape,
        compiler_params=pltpu.CompilerParams(vmem_limit_bytes=VMEM_LIMIT),
    )(a_re, a_im, log_dt.reshape(g, 1), rep(a_re), rep(a_im), b_re.reshape(g, p * c), b_im.reshape(g, p * c))
    eye = jnp.eye(g, dtype=F32)
    blockdiag_in = lambda bb: jnp.einsum('gpc,gh->gchp', bb.reshape(g, p, c), eye).reshape(g * c, g * p)
    blockdiag_out = lambda cc: jnp.einsum('gcp,gh->gphc', cc, eye).reshape(g * p, g * c)
    bmat = jnp.concatenate([blockdiag_in(bbre), blockdiag_in(bbim)], axis=1)
    cmat = jnp.concatenate([blockdiag_out(c_re), blockdiag_out(-c_im)], axis=0)
    return bmat, cmat, pre.reshape(SSM_SEG, g * p), pim.reshape(SSM_SEG, g * p)


def _ssm_out(h_all, u, cmat_ref, d_ref, wglu_ref):
    y = _dot(h_all.astype(BF16), cmat_ref[...]) + d_ref[...] * u
    z = jax.nn.gelu(y)
    return z * jax.nn.sigmoid(_dot(z.astype(BF16), wglu_ref[...]))


def _ssm_prompt_kernel(u_ref, bmat_ref, cmat_ref, pre_ref, pim_ref, d_ref, wglu_ref,
                       o_ref, hre_ref, him_ref, h_ref, carry_ref):
    n = SSM_LANES

    @pl.when(pl.program_id(0) == 0)
    def _():
        carry_ref[...] = jnp.zeros(carry_ref.shape, F32)

    u = u_ref[...]
    bu = _dot(u.astype(BF16), bmat_ref[...])
    n_blk = n // LANES
    for b in range(2 * n_blk):
        h_ref[b] = bu[:, b * LANES:(b + 1) * LANES]

    group = 4
    for b0 in range(0, n_blk, group):
        a = [(jnp.broadcast_to(pre_ref[0:1, pl.ds(b * LANES, LANES)], (SUBLANES, LANES)),
              jnp.broadcast_to(pim_ref[0:1, pl.ds(b * LANES, LANES)], (SUBLANES, LANES)))
             for b in range(b0, b0 + group)]

        def step(j, carry):
            rows = pl.ds(j, SUBLANES, stride=SSM_SEG)
            out = []
            for g in range(group):
                (a_re, a_im), (h_re, h_im) = a[g], carry[g]
                n_re = a_re * h_re - a_im * h_im + h_ref[b0 + g, rows, :]
                n_im = a_re * h_im + a_im * h_re + h_ref[n_blk + b0 + g, rows, :]
                h_ref[b0 + g, rows, :] = n_re
                h_ref[n_blk + b0 + g, rows, :] = n_im
                out.append((n_re, n_im))
            return tuple(out)

        zero = jnp.zeros((SUBLANES, LANES), F32)
        lax.fori_loop(0, SSM_SEG, step, ((zero, zero),) * group)

    for b in range(n_blk):
        re_sl, im_sl = pl.ds(b * LANES, LANES), pl.ds(n + b * LANES, LANES)
        p_re, p_im = pre_ref[:, re_sl], pim_ref[:, re_sl]
        f_re, f_im = carry_ref[:, re_sl], carry_ref[:, im_sl]
        for s in range(SUBLANES):
            rows = pl.ds(s * SSM_SEG, SSM_SEG)
            n_re = h_ref[b, rows, :] + p_re * f_re - p_im * f_im
            n_im = h_ref[n_blk + b, rows, :] + p_re * f_im + p_im * f_re
            h_ref[b, rows, :] = n_re
            h_ref[n_blk + b, rows, :] = n_im
            f_re, f_im = n_re[SSM_SEG - 1:, :], n_im[SSM_SEG - 1:, :]
        carry_ref[:, re_sl] = f_re
        carry_ref[:, im_sl] = f_im

    h_all = jnp.concatenate([h_ref[b] for b in range(2 * n_blk)], axis=1)
    o_ref[...] = _ssm_out(h_all, u, cmat_ref, d_ref, wglu_ref)
    hre_ref[...] = carry_ref[:, :n]
    him_ref[...] = carry_ref[:, n:]


def ssm_prompt(u, tables, d, w_glu):
    t = u.shape[0]
    bmat, cmat, pre, pim = tables
    n = SSM_LANES
    ins = [u, bmat.astype(BF16), cmat.astype(BF16), pre, pim, d.reshape(1, -1), w_glu.astype(BF16)]
    return pl.pallas_call(
        _ssm_prompt_kernel,
        grid=(t // SSM_CHUNK,),
        in_specs=[pl.BlockSpec((SSM_CHUNK, SSM_W), lambda i: (i, 0))] + [_full(a.shape) for a in ins[1:]],
        out_specs=[pl.BlockSpec((SSM_CHUNK, SSM_W), lambda i: (i, 0)),
                   pl.BlockSpec((1, n), lambda i: (0, 0)), pl.BlockSpec((1, n), lambda i: (0, 0))],
        out_shape=[jax.ShapeDtypeStruct((t, SSM_W), F32),
                   jax.ShapeDtypeStruct((1, n), F32), jax.ShapeDtypeStruct((1, n), F32)],
        scratch_shapes=[pltpu.VMEM((2 * n // LANES, SSM_CHUNK, LANES), F32), pltpu.VMEM((1, 2 * n), F32)],
        compiler_params=_params("arbitrary"),
    )(*ins)


def _ssm_sample_kernel(u_ref, h0re_ref, h0im_ref, bmat_ref, cmat_ref, pre_ref, pim_ref, d_ref, wglu_ref,
                       o_ref, hre_ref, him_ref):
    n = SSM_LANES
    u = u_ref[...]
    bu = jnp.dot(u, bmat_ref[...], preferred_element_type=F32, precision=lax.Precision.HIGHEST)
    a_re, a_im = pre_ref[0:1, :], pim_ref[0:1, :]
    h0_re, h0_im = h0re_ref[...], h0im_ref[...]
    h_re = bu[:, :n] + a_re * h0_re - a_im * h0_im
    h_im = bu[:, n:] + a_re * h0_im + a_im * h0_re
    hre_ref[...] = h_re
    him_ref[...] = h_im
    o_ref[...] = _ssm_out(jnp.concatenate([h_re, h_im], axis=1), u, cmat_ref, d_ref, wglu_ref)


def ssm_sample(u, h0_re, h0_im, tables, d, w_glu):
    b = u.shape[0]
    bmat, cmat, pre, pim = tables
    n = SSM_LANES
    return pl.pallas_call(
        _ssm_sample_kernel,
        out_shape=[jax.ShapeDtypeStruct((b, SSM_W), F32),
                   jax.ShapeDtypeStruct((b, n), F32), jax.ShapeDtypeStruct((b, n), F32)],
        compiler_params=pltpu.CompilerParams(vmem_limit_bytes=VMEM_LIMIT),
    )(u, h0_re.reshape(b, n), h0_im.reshape(b, n), bmat, cmat.astype(BF16), pre, pim,
      d.reshape(1, -1), w_glu.astype(BF16))


def _merge_kernel(x_ref, gate_ref, att_ref, ssm_ref, mem_ref, wa_ref, ws_ref, wm_ref, wo_ref,
                  nf_ref, wq_ref, skt_ref, h_out, hn_out, st_out):
    g = gate_ref[...]
    mixed = (g[:, :D_MODEL] * _dot(att_ref[...].astype(BF16), wa_ref[...])
             + g[:, D_MODEL:2 * D_MODEL] * _dot(ssm_ref[...].astype(BF16), ws_ref[...])
             + g[:, 2 * D_MODEL:] * _dot(mem_ref[...].astype(BF16), wm_ref[...]))
    h = x_ref[...] + _dot(mixed.astype(BF16), wo_ref[...])
    hn = _rms(h, nf_ref[...], D_MODEL).astype(BF16)
    pq = _dot(hn, wq_ref[...]).astype(BF16)
    h_out[...] = h
    hn_out[...] = hn
    st_out[...] = _dot_nt(skt_ref[...], pq)


def _sub_key_matrix(sub_keys):
    nb = PEER_HEADS * 2
    sk = sub_keys.reshape(nb, N_KEYS, PEER_DK // 2)
    eye = jnp.eye(nb, dtype=sk.dtype)
    return jnp.einsum('bnd,bc->bncd', sk, eye).reshape(nb * N_KEYS, nb * (PEER_DK // 2))


def merge(x, gate, att_o, ssm_o, mem_o, w_br_attn, w_br_ssm, w_br_mem, w_out, norm_ffn, peer_w_q,
          sub_keys, tile=256):
    t = x.shape[0]
    ws = [w.astype(BF16) for w in (w_br_attn, w_br_ssm, w_br_mem, w_out)]
    ins = [x, gate, att_o, ssm_o, mem_o, *ws, norm_ffn.reshape(1, -1), peer_w_q.astype(BF16),
           _sub_key_matrix(sub_keys).astype(BF16)]
    row = lambda w: pl.BlockSpec((tile, w), lambda i: (i, 0))
    n_scores = PEER_HEADS * 2 * N_KEYS
    return pl.pallas_call(
        _merge_kernel,
        grid=(t // tile,),
        in_specs=[row(D_MODEL), row(N_BRANCH * D_MODEL), row(ATT_W), row(SSM_W), row(MEM_W)]
        + [_full(a.shape) for a in ins[5:]],
        out_specs=[row(D_MODEL), row(D_MODEL), pl.BlockSpec((n_scores, tile), lambda i: (0, i))],
        out_shape=[jax.ShapeDtypeStruct((t, D_MODEL), F32), jax.ShapeDtypeStruct((t, D_MODEL), BF16),
                   jax.ShapeDtypeStruct((n_scores, t), F32)],
        compiler_params=_params("parallel"),
    )(*ins)


PEER_TOKENS = 256
PEER_CHUNK = 2048
NEG_INF = float("-inf")


def _top_values(x, k):
    vals = []
    for _ in range(k):
        m = jnp.max(x, axis=0, keepdims=True)
        vals.append(m)
        x = jnp.where(x == m, NEG_INF, x)
    return vals


def _peer_route(st_ref, s1_ref, s2_ref, e2_ref, f1_ref, thr_ref):
    tokens = st_ref.shape[1]
    for h in range(PEER_HEADS):
        s1 = st_ref[pl.ds(2 * h * N_KEYS, N_KEYS), :]
        s2 = st_ref[pl.ds((2 * h + 1) * N_KEYS, N_KEYS), :]
        a = _top_values(s1, PEER_TOPK)
        b = _top_values(s2, PEER_TOPK)
        cand = [a[i] + b[j] for i in range(PEER_TOPK) for j in range(PEER_TOPK) if (i + 1) * (j + 1) <= PEER_TOPK]
        pad = -len(cand) % SUBLANES
        cand = jnp.concatenate(cand + [jnp.full((pad, tokens), NEG_INF, F32)], axis=0)
        thr = _top_values(cand, PEER_TOPK)[-1]
        top = a[0] + b[0]
        z = jnp.sum(jnp.where(cand >= thr, jnp.exp(cand - top), 0.0), axis=0, keepdims=True)
        s1_ref[h] = s1
        s2_ref[h] = s2
        e2_ref[h] = jnp.exp(s2 - b[0])
        f1_ref[h] = jnp.exp(s1 - a[0]) / z
        thr_ref[h] = jnp.broadcast_to(thr, (SUBLANES, tokens))


def _peer_kernel(hn_ref, st_ref, h_ref, u_ref, vt_ref, y_ref,
                 s1_ref, s2_ref, e2_ref, f1_ref, thr_ref, acc_ref, pt_ref):
    e = pl.program_id(1)
    tokens = hn_ref.shape[0]
    rows_per_step = PEER_CHUNK // N_KEYS

    @pl.when(e == 0)
    def _():
        _peer_route(st_ref, s1_ref, s2_ref, e2_ref, f1_ref, thr_ref)
        acc_ref[...] = jnp.zeros(acc_ref.shape, F32)

    for il in range(rows_per_step):
        i = e * rows_per_step + il
        sl = slice(il * N_KEYS, (il + 1) * N_KEYS)
        w = jnp.zeros((N_KEYS, tokens), F32)
        for h in range(PEER_HEADS):
            pair = s2_ref[h] + s1_ref[h, pl.ds(i, 1), :]
            val = e2_ref[h] * f1_ref[h, pl.ds(i, 1), :]
            w = w + jnp.where(pair >= thr_ref[h, 0:1, :], val, 0.0)
        act = jax.nn.gelu(_dot_nt(u_ref[sl, :], hn_ref[...]))
        pt_ref[sl, :] = (w * act).astype(BF16)
    acc_ref[...] += _dot(vt_ref[...], pt_ref[...])

    @pl.when(e == pl.num_programs(1) - 1)
    def _():
        y_ref[...] = h_ref[...] + acc_ref[...].T


def peer(hn, st, h, u_bf, vt_bf):
    t = hn.shape[0]
    tb = PEER_TOKENS
    n_scores = st.shape[0]
    return pl.pallas_call(
        _peer_kernel,
        grid=(t // tb, N_EXPERTS // PEER_CHUNK),
        in_specs=[pl.BlockSpec((tb, D_MODEL), lambda i, e: (i, 0)),
                  pl.BlockSpec((n_scores, tb), lambda i, e: (0, i)),
                  pl.BlockSpec((tb, D_MODEL), lambda i, e: (i, 0)),
                  pl.BlockSpec((PEER_CHUNK, D_MODEL), lambda i, e: (e, 0)),
                  pl.BlockSpec((D_MODEL, PEER_CHUNK), lambda i, e: (0, e))],
        out_specs=pl.BlockSpec((tb, D_MODEL), lambda i, e: (i, 0)),
        out_shape=jax.ShapeDtypeStruct((t, D_MODEL), F32),
        scratch_shapes=[pltpu.VMEM((PEER_HEADS, N_KEYS, tb), F32)] * 4
        + [pltpu.VMEM((PEER_HEADS, SUBLANES, tb), F32),
           pltpu.VMEM((D_MODEL, tb), F32), pltpu.VMEM((PEER_CHUNK, tb), BF16)],
        compiler_params=_params("parallel", "arbitrary"),
    )(hn, st, h, u_bf, vt_bf)


def _bf16_round(x):
    return x.astype(BF16).astype(F32)


def _attend_rows(q, kb, vb):
    s = jnp.sum(_bf16_round(kb) * _bf16_round(q)[None], axis=2, keepdims=True)
    p = jnp.exp(s - jnp.max(s, axis=0, keepdims=True))
    p = p / jnp.sum(p, axis=0, keepdims=True)
    return jnp.sum(_bf16_round(p) * _bf16_round(vb), axis=0)


def _mem_sample_kernel(qm_ref, mk_ref, mv_ref, o_ref):
    o_ref[0] = _attend_rows(qm_ref[0] * (MEM_HD ** -0.5), mk_ref[0], mv_ref[0])


def mem_attend_sample(qm, cache_mem_k, cache_mem_v):
    b = qm.shape[0]
    blk = pl.BlockSpec((1, MEM_HEADS, MEM_HD), lambda i: (i, 0, 0))
    cache = pl.BlockSpec((1, MEM_TOKENS, MEM_HEADS, MEM_HD), lambda i: (i, 0, 0, 0))
    return pl.pallas_call(
        _mem_sample_kernel, grid=(b,), in_specs=[blk, cache, cache], out_specs=blk,
        out_shape=jax.ShapeDtypeStruct((b, MEM_HEADS, MEM_HD), F32),
        compiler_params=_params("parallel"),
    )(qm.reshape(b, MEM_HEADS, MEM_HD), cache_mem_k, cache_mem_v)


PAGE_GROUP = 8
RING_SLOTS = 3
SCORE_ROWS = 256


def _select_tile(keys, topk):
    rows = keys.shape[0]

    def count_ge(cand):
        return jnp.sum(jnp.sum(jnp.where(keys >= cand, 1.0, 0.0), axis=0, keepdims=True), axis=1, keepdims=True)

    thr = jnp.maximum(_kth_largest_key(count_ge, 1, float(topk)), INT_MIN + 1)
    need = float(topk) - count_ge(thr + 1)
    tie = jnp.where(keys == thr, 1.0, 0.0)
    ri = lax.broadcasted_iota(I32, (LANES, LANES), 0)
    ci = lax.broadcasted_iota(I32, (LANES, LANES), 1)
    within = _dot(tie.astype(BF16), jnp.where(ri <= ci, 1.0, 0.0).astype(BF16))
    row_tot = jnp.broadcast_to(jnp.sum(tie, axis=1, keepdims=True), (rows, LANES))
    rr = lax.broadcasted_iota(I32, (rows, rows), 0)
    rc = lax.broadcasted_iota(I32, (rows, rows), 1)
    before = _dot(jnp.where(rc < rr, 1.0, 0.0).astype(BF16), row_tot.astype(BF16))
    take_tie = jnp.where(within + before <= need, tie, 0.0)
    return jnp.where(keys > thr, 1.0, take_tie)


def _sample_attn_kernel(pt_ref, q_ref, qi_ref, wi_ref, kin_ref, kn_ref, vn_ref, cki_ref, ck_ref, cv_ref,
                        o_ref, kibuf, ring, qb_ref, qib_ref, key_ref, bias_ref, satt_ref, acc_ref,
                        sem_ki, sem_ring, *, topk):
    b, nb = pl.program_id(0), pl.num_programs(0)
    n_pages = pt_ref.shape[1]
    groups = n_pages // PAGE_GROUP
    n_stream = nb * 2 * groups
    par = b % 2

    def ki_copy(bb, buf, j):
        return pltpu.make_async_copy(cki_ref.at[pt_ref[bb, j]], kibuf.at[buf, j], sem_ki.at[buf])

    def start_ki(bb, buf):
        lax.fori_loop(0, n_pages, lambda j, c: (ki_copy(bb, buf, j).start(), c)[1], 0)

    def group_copy(cache_ref, bb, g, slot, i):
        return pltpu.make_async_copy(cache_ref.at[pt_ref[bb, g * PAGE_GROUP + i]], ring.at[slot, i],
                                     sem_ring.at[slot])

    def start_group(n):
        bb, g, slot = n // (2 * groups), n % (2 * groups), n % RING_SLOTS

        @pl.when(g < groups)
        def _():
            for i in range(PAGE_GROUP):
                group_copy(ck_ref, bb, g, slot, i).start()

        @pl.when(g >= groups)
        def _():
            for i in range(PAGE_GROUP):
                group_copy(cv_ref, bb, g - groups, slot, i).start()

    def next_group(n):
        slot = n % RING_SLOTS
        for i in range(PAGE_GROUP):
            group_copy(ck_ref, 0, 0, slot, i).wait()

        @pl.when(n + RING_SLOTS - 1 < n_stream)
        def _():
            start_group(n + RING_SLOTS - 1)

        return slot

    @pl.when(b == 0)
    def _():
        start_ki(0, 0)
        for n in range(RING_SLOTS - 1):
            start_group(n)

    @pl.when(b + 1 < nb)
    def _():
        start_ki(b + 1, 1 - par)

    lax.fori_loop(0, n_pages, lambda j, c: (ki_copy(b, par, j).wait(), c)[1], 0)

    lanes = (IDX_DIM, LANES)
    for h in range(ATT_HEADS):
        qb_ref[h] = jnp.broadcast_to(q_ref[0, h], lanes)
        qib_ref[h] = jnp.broadcast_to(qi_ref[0, h], lanes)
    w = wi_ref[0]
    lane = lax.broadcasted_iota(I32, (1, LANES), 1)

    def head_dots(page_of_head, qsrc):
        return jnp.concatenate([jnp.sum(page_of_head(h) * qsrc[h], axis=0, keepdims=True)
                                for h in range(ATT_HEADS)], axis=0)

    def index_row(page):
        s = head_dots(lambda h: page, qib_ref)
        return _sortable(jnp.sum(w * jnp.maximum(s, 0.0), axis=0, keepdims=True))

    key_ref[...] = jnp.full(key_ref.shape, INT_MIN, I32)

    def score_page(j, c):
        key_ref[pl.ds(j, 1), :] = index_row(kibuf[par, j])
        return c

    lax.fori_loop(0, n_pages, score_page, 0)
    new_row = index_row(jnp.broadcast_to(kin_ref[0], lanes))
    key_ref[pl.ds(n_pages, 1), :] = jnp.where(lane == 0, new_row, INT_MIN)
    bias_ref[...] = jnp.where(_select_tile(key_ref[...], topk) > 0.0, 0.0, MASKED_SCORE)

    def att_row(page, j):
        satt_ref[j] = head_dots(lambda h: page[h], qb_ref) + bias_ref[pl.ds(j, 1), :]

    def k_group(g, c):
        slot = next_group(b * 2 * groups + g)
        for i in range(PAGE_GROUP):
            att_row(ring[slot, i], g * PAGE_GROUP + i)
        return c

    lax.fori_loop(0, groups, k_group, 0)
    k_new = [jnp.broadcast_to(kn_ref[0, h], lanes) for h in range(ATT_HEADS)]
    att_row(k_new, n_pages)

    m = lax.fori_loop(0, n_pages + 1, lambda j, m: jnp.maximum(m, satt_ref[j]),
                      jnp.full((ATT_HEADS, LANES), MASKED_SCORE, F32))
    m = jnp.max(m, axis=1, keepdims=True)

    acc_ref[...] = jnp.zeros(acc_ref.shape, F32)

    def add_page(page, j, psum):
        p = jnp.exp(satt_ref[j] - m)
        for h in range(ATT_HEADS):
            acc_ref[h] += page[h] * p[h:h + 1, :]
        return psum + p

    def v_group(g, psum):
        slot = next_group(b * 2 * groups + groups + g)
        for i in range(PAGE_GROUP):
            psum = add_page(ring[slot, i], g * PAGE_GROUP + i, psum)
        return psum

    psum = lax.fori_loop(0, groups, v_group, jnp.zeros((ATT_HEADS, LANES), F32))
    v_new = [jnp.broadcast_to(vn_ref[0, h], lanes) for h in range(ATT_HEADS)]
    psum = add_page(v_new, n_pages, psum)
    denom = jnp.sum(psum, axis=1, keepdims=True)
    for h in range(ATT_HEADS):
        o_ref[0, h] = jnp.sum(acc_ref[h], axis=1, keepdims=True) / denom[h:h + 1, :]


def sample_attention(q, qi, wi, ki_new, k_new, v_new, cache_idx_k, cache_k, cache_v, page_table, topk):
    b = q.shape[0]
    n_pages = page_table.shape[1]
    col = lambda a, h: a.reshape(b, h, -1, 1)
    cki = jnp.transpose(cache_idx_k, (0, 2, 1))
    ck = jnp.transpose(cache_k, (0, 2, 3, 1))
    cv = jnp.transpose(cache_v, (0, 2, 3, 1))
    head_col = pl.BlockSpec((1, ATT_HEADS, HEAD_DIM, 1), lambda i, pt: (i, 0, 0, 0))
    any_spec = pl.BlockSpec(memory_space=pl.ANY)
    grid_spec = pltpu.PrefetchScalarGridSpec(
        num_scalar_prefetch=1, grid=(b,),
        in_specs=[head_col, head_col,
                  pl.BlockSpec((1, IDX_HEADS, 1), lambda i, pt: (i, 0, 0)),
                  pl.BlockSpec((1, IDX_DIM, 1), lambda i, pt: (i, 0, 0)),
                  head_col, head_col, any_spec, any_spec, any_spec],
        out_specs=head_col,
        scratch_shapes=[pltpu.VMEM((2, n_pages, IDX_DIM, PAGE_SIZE), F32),
                        pltpu.VMEM((RING_SLOTS, PAGE_GROUP, ATT_HEADS, HEAD_DIM, PAGE_SIZE), F32),
                        pltpu.VMEM((ATT_HEADS, HEAD_DIM, LANES), F32),
                        pltpu.VMEM((IDX_HEADS, IDX_DIM, LANES), F32),
                        pltpu.VMEM((SCORE_ROWS, LANES), I32),
                        pltpu.VMEM((SCORE_ROWS, LANES), F32),
                        pltpu.VMEM((n_pages + 1, ATT_HEADS, LANES), F32),
                        pltpu.VMEM((ATT_HEADS, HEAD_DIM, LANES), F32),
                        pltpu.SemaphoreType.DMA((2,)), pltpu.SemaphoreType.DMA((RING_SLOTS,))])
    out = pl.pallas_call(
        functools.partial(_sample_attn_kernel, topk=topk), grid_spec=grid_spec,
        out_shape=jax.ShapeDtypeStruct((b, ATT_HEADS, HEAD_DIM, 1), F32),
        compiler_params=_params("arbitrary"),
    )(page_table, col(q, ATT_HEADS), col(qi, IDX_HEADS), wi[:, :IDX_HEADS].reshape(b, IDX_HEADS, 1),
      ki_new.reshape(b, IDX_DIM, 1), col(k_new, ATT_HEADS), col(v_new, ATT_HEADS), cki, ck, cv)
    return out.reshape(b, ATT_W)


def _pad_rows(a, rows):
    return jnp.pad(a, ((0, rows - a.shape[0]), (0, 0)))


def kernel(x_prompt, x_sample, cache_k, cache_v, cache_idx_k, cache_mem_k, cache_mem_v, state_ssm_re, state_ssm_im, page_table, mem_prompt, norm_mix, w_in, q_norm, k_norm, idx_k_norm, mq_norm, ssm_a_re, ssm_a_im, ssm_b_re, ssm_b_im, ssm_c_re, ssm_c_im, ssm_d, ssm_log_dt, w_glu, mem_norm, w_mem_kv, mk_norm, w_br_attn, w_br_ssm, w_br_mem, w_out, norm_ffn, peer_w_q, peer_sub_keys, peer_u, peer_v):
    seq = x_prompt.shape[1]
    bd = x_sample.shape[0]
    w_parts = _split_w_in(w_in)
    tables = ssm_tables(ssm_a_re, ssm_a_im, ssm_b_re, ssm_b_im, ssm_c_re, ssm_c_im, ssm_log_dt)
    u_bf = peer_u.astype(BF16)
    vt_bf = peer_v.T.astype(BF16)
    tail = (w_br_attn, w_br_ssm, w_br_mem, w_out, norm_ffn, peer_w_q, peer_sub_keys)
    norms = (q_norm, k_norm, idx_k_norm, mq_norm)

    mk, mv = mem_kv(mem_prompt[0], mem_norm, w_mem_kv, mk_norm)
    (k_p, v_p, ki_p, wi, u, gate, q_ext, qi_ext, k_bf, v_bf, ki_dup, mem_o) = project(
        x_prompt[0], jnp.arange(seq), norm_mix, w_parts, *norms, mem=(mk, mv))
    att_o = attn_prompt(qi_ext, q_ext, wi, ki_dup, k_bf, v_bf)
    ssm_o, hre_p, him_p = ssm_prompt(u, tables, ssm_d, w_glu)
    h, hn, st = merge(x_prompt[0], gate, att_o, ssm_o, mem_o, *tail)
    y_prompt = peer(hn, st, h, u_bf, vt_bf)

    (k_s, v_s, ki_s, wi, u, gate, q, qi, qm) = project(
        x_sample[:, 0], jnp.full((bd,), PAST_LEN), norm_mix, w_parts, *norms)
    att_o = sample_attention(q, qi, wi, ki_s, k_s, v_s, cache_idx_k, cache_k, cache_v, page_table,
                             min(IDX_TOPK_MAX, (PAST_LEN + 1) // 4))
    ssm_o, hre_s, him_s = ssm_sample(u, state_ssm_re, state_ssm_im, tables, ssm_d, w_glu)
    mem_o = mem_attend_sample(qm, cache_mem_k, cache_mem_v).reshape(bd, MEM_W)
    pad = lambda a: _pad_rows(a, PEER_TOKENS)
    h, hn, st = merge(pad(x_sample[:, 0]), pad(gate), pad(att_o), pad(ssm_o), pad(mem_o), *tail)
    y_sample = peer(hn, st, h, u_bf, vt_bf)[:bd]

    heads = lambda a: a.reshape(a.shape[0], ATT_HEADS, HEAD_DIM)
    state = lambda a, n: a.reshape(n, SSM_GROUPS, SSM_STATE)
    return (y_prompt[None], y_sample[:, None],
            heads(k_p)[None], heads(v_p)[None], ki_p[None],
            mk.reshape(1, MEM_TOKENS, MEM_HEADS, MEM_HD), mv.reshape(1, MEM_TOKENS, MEM_HEADS, MEM_HD),
            state(hre_p, 1), state(him_p, 1),
            heads(k_s)[:, None], heads(v_s)[:, None], ki_s[:, None],
            state(hre_s, bd), state(him_s, bd))
```

```python
import functools
import math

import jax
import jax.numpy as jnp
import numpy as np
from jax import lax
from jax.experimental import pallas as pl
from jax.experimental.pallas import tpu as pltpu

F32 = jnp.float32
BF16 = jnp.bfloat16
I32 = jnp.int32

D_MODEL = 1024
PAST_LEN = 16384
PAGE_SIZE = 128
ATT_HEADS = 8
HEAD_DIM = 64
ATT_W = ATT_HEADS * HEAD_DIM
IDX_HEADS = 8
IDX_DIM = 64
IDX_TOPK_MAX = 256
ROPE_THETA = 500000.0
ROPE_HALF = HEAD_DIM // 4 // 2
SSM_GROUP = 16
SSM_W = 512
SSM_GROUPS = SSM_W // SSM_GROUP
SSM_STATE = 64
SSM_LANES = SSM_GROUPS * SSM_STATE
MEM_TOKENS = 256
MEM_HEADS = 4
MEM_HD = 128
MEM_W = MEM_HEADS * MEM_HD
N_BRANCH = 3
PEER_HEADS = 8
N_KEYS = 128
N_EXPERTS = N_KEYS * N_KEYS
PEER_DK = 128
PEER_TOPK = 16
EPS = 1e-6

LANES = 128
SUBLANES = 8
VMEM_LIMIT = 56 * 1024 * 1024
INT_MIN = -(2 ** 31)
MASKED_SCORE = -1e30
RUNNING_MAX_INIT = -5e29

NT_DIMS = (((1,), (1,)), ((), ()))


def _params(*sem):
    return pltpu.CompilerParams(dimension_semantics=sem, vmem_limit_bytes=VMEM_LIMIT)


def _full(shape):
    nd = len(shape)
    return pl.BlockSpec(shape, lambda *_: (0,) * nd, pipeline_mode=pl.Buffered(1))


def _rms(x, g, n):
    ms = jnp.sum(x * x, axis=-1, keepdims=True) * (1.0 / n)
    return x * lax.rsqrt(ms + EPS) * g


def _dot(a, b):
    return jnp.dot(a, b, preferred_element_type=F32)


def _dot_nt(a, b):
    return lax.dot_general(a, b, NT_DIMS, preferred_element_type=F32)


def _mem_kv_kernel(mem_ref, mn_ref, w_ref, mkn_ref, mk_ref, mv_ref):
    xn = _rms(mem_ref[...], mn_ref[...], D_MODEL).astype(BF16)
    y = _dot(xn, w_ref[...])
    for h in range(MEM_HEADS):
        sl = slice(h * MEM_HD, (h + 1) * MEM_HD)
        mk_ref[:, sl] = _rms(y[:, sl], mkn_ref[...], MEM_HD)
    mv_ref[...] = y[:, MEM_W:]


def mem_kv(mem, mem_norm, w_mem_kv, mk_norm):
    m = mem.shape[0]
    return pl.pallas_call(
        _mem_kv_kernel,
        out_shape=(jax.ShapeDtypeStruct((m, MEM_W), F32), jax.ShapeDtypeStruct((m, MEM_W), F32)),
        compiler_params=pltpu.CompilerParams(vmem_limit_bytes=VMEM_LIMIT),
    )(mem, mem_norm.reshape(1, -1), w_mem_kv.astype(BF16), mk_norm.reshape(1, -1))


def _rope_tables(pos):
    inv_freq = ROPE_THETA ** (-jnp.arange(ROPE_HALF, dtype=F32) / ROPE_HALF)
    ang = pos.astype(F32)[:, None] * inv_freq[None, :]
    cos, sin = jnp.cos(ang), jnp.sin(ang)
    t = pos.shape[0]
    rest = HEAD_DIM - 2 * ROPE_HALF
    c = jnp.concatenate([cos, cos, jnp.ones((t, rest), F32)], axis=1)
    s_hi = jnp.concatenate([-sin, jnp.zeros((t, HEAD_DIM - ROPE_HALF), F32)], axis=1)
    s_lo = jnp.concatenate([jnp.zeros((t, ROPE_HALF), F32), sin, jnp.zeros((t, rest), F32)], axis=1)
    rep = LANES // HEAD_DIM
    return jnp.tile(c, (1, rep)), jnp.tile(s_hi, (1, rep)), jnp.tile(s_lo, (1, rep))


def _rope(y, c, s_hi, s_lo):
    w = y.shape[1]
    rep = w // LANES
    if rep > 1:
        c = jnp.concatenate([c] * rep, axis=1)
        s_hi = jnp.concatenate([s_hi] * rep, axis=1)
        s_lo = jnp.concatenate([s_lo] * rep, axis=1)
    return y * c + pltpu.roll(y, w - ROPE_HALF, axis=1) * s_hi + pltpu.roll(y, ROPE_HALF, axis=1) * s_lo


def _group_rms(y, grp, g, n):
    sq = y * y
    hi = sq.astype(BF16)
    lo = (sq - hi.astype(F32)).astype(BF16)
    ss = _dot(hi, grp) + _dot(lo, grp)
    return y * lax.rsqrt(ss * (1.0 / n) + EPS) * g


def _proj_kernel(*refs, prompt):
    (x_ref, nm_ref, wq_ref, wk_ref, wv_ref, wqi_ref, wki_ref, wwi_ref, wu_ref, wqm_ref, wg_ref,
     qn_ref, kn_ref, ikn_ref, mqn_ref, grp_ref, c_ref, shi_ref, slo_ref) = refs[:19]
    refs = refs[19:]
    if prompt:
        mk_ref, mv_ref = refs[:2]
        (k_out, v_out, ki_out, wi_out, u_out, gate_out,
         qx_out, qix_out, kbf_out, vbf_out, kid_out, memo_out) = refs[2:]
    else:
        k_out, v_out, ki_out, wi_out, u_out, gate_out, q_out, qi_out, qm_out = refs

    xn = _rms(x_ref[...], nm_ref[...], D_MODEL).astype(BF16)
    c, s_hi, s_lo = c_ref[...], shi_ref[...], slo_ref[...]
    grp = grp_ref[...]

    q = _rope(_group_rms(_dot(xn, wq_ref[...]), grp, qn_ref[...], HEAD_DIM), c, s_hi, s_lo)
    q = q * (HEAD_DIM ** -0.5)
    k = _rope(_group_rms(_dot(xn, wk_ref[...]), grp, kn_ref[...], HEAD_DIM), c, s_hi, s_lo)
    v = _dot(xn, wv_ref[...])
    qi = _rope(_dot(xn, wqi_ref[...]), c, s_hi, s_lo)
    ki2 = _dot(xn, wki_ref[...])
    ki2 = _rope(_rms(ki2, ikn_ref[...], 2 * IDX_DIM), c, s_hi, s_lo)
    wi = _dot(xn, wwi_ref[...]) * (IDX_HEADS ** -0.5 * IDX_DIM ** -0.5)
    qm = _dot(xn, wqm_ref[...])
    qm = jnp.concatenate(
        [_rms(qm[:, h * MEM_HD:(h + 1) * MEM_HD], mqn_ref[...], MEM_HD) for h in range(MEM_HEADS)], axis=1)

    k_out[...] = k
    v_out[...] = v
    ki_out[...] = ki2[:, :IDX_DIM]
    wi_out[...] = wi
    u_out[...] = _dot(xn, wu_ref[...])
    gate_out[...] = jax.nn.sigmoid(_dot(xn, wg_ref[...]))

    if not prompt:
        q_out[...] = q
        qi_out[...] = qi
        qm_out[...] = qm
        return

    kbf_out[...] = k.astype(BF16)
    vbf_out[...] = v.astype(BF16)
    kid_out[...] = ki2.astype(BF16)
    lane = lax.broadcasted_iota(I32, (q.shape[0], LANES), 1)
    for h in range(ATT_HEADS):
        pair = slice((h // 2) * LANES, (h // 2 + 1) * LANES)
        own = (lane < HEAD_DIM) if h % 2 == 0 else (lane >= HEAD_DIM)
        qx_out[h] = jnp.where(own, q[:, pair], 0.0).astype(BF16)
        qix_out[h] = jnp.where(own, qi[:, pair], 0.0).astype(BF16)

    for h in range(MEM_HEADS):
        sl = slice(h * MEM_HD, (h + 1) * MEM_HD)
        s = _dot_nt(qm[:, sl].astype(BF16), mk_ref[:, sl]) * (MEM_HD ** -0.5)
        p = jnp.exp(s - jnp.max(s, axis=-1, keepdims=True))
        p = p / jnp.sum(p, axis=-1, keepdims=True)
        memo_out[:, sl] = _dot(p.astype(BF16), mv_ref[:, sl])


def _split_w_in(w_in):
    splits = np.cumsum([ATT_W, ATT_W, ATT_W, IDX_HEADS * IDX_DIM, IDX_DIM, IDX_HEADS, SSM_W, MEM_W])
    wq, wk, wv, wqi, wki, wwi, wu, wqm, wg = jnp.split(w_in.astype(BF16), splits.tolist(), axis=1)
    wki = jnp.concatenate([wki, wki], axis=1)
    wwi = jnp.pad(wwi, ((0, 0), (0, LANES - IDX_HEADS)))
    return wq, wk, wv, wqi, wki, wwi, wu, wqm, wg


def project(x, pos, norm_mix, w_parts, q_norm, k_norm, idx_k_norm, mq_norm, mem=None, tile=256):
    t = x.shape[0]
    tile = min(tile, t)
    prompt = mem is not None
    heads_per_slab = ATT_W // HEAD_DIM
    grp = jnp.kron(jnp.eye(heads_per_slab, dtype=F32), jnp.ones((HEAD_DIM, HEAD_DIM), F32)).astype(BF16)
    c, s_hi, s_lo = _rope_tables(pos)
    tile_rep = lambda g, n: jnp.tile(g.reshape(1, -1), (1, n))
    ins = [x, norm_mix.reshape(1, -1), *w_parts,
           tile_rep(q_norm, ATT_HEADS), tile_rep(k_norm, ATT_HEADS), tile_rep(idx_k_norm, 2),
           mq_norm.reshape(1, -1), grp, c, s_hi, s_lo]
    row = lambda w: pl.BlockSpec((tile, w), lambda i: (i, 0))
    in_specs = [row(D_MODEL)] + [_full(a.shape) for a in ins[1:16]] + [row(LANES)] * 3
    outs = [(ATT_W, F32), (ATT_W, F32), (IDX_DIM, F32), (LANES, F32), (SSM_W, F32), (N_BRANCH * D_MODEL, F32)]
    out_shape = [jax.ShapeDtypeStruct((t, w), d) for w, d in outs]
    out_specs = [row(w) for w, _ in outs]
    if prompt:
        mk, mv = mem
        ins += [mk.astype(BF16), mv.astype(BF16)]
        in_specs += [_full(mk.shape), _full(mv.shape)]
        ext = pl.BlockSpec((ATT_HEADS, tile, LANES), lambda i: (0, i, 0))
        out_shape += [jax.ShapeDtypeStruct((ATT_HEADS, t, LANES), BF16)] * 2
        out_specs += [ext, ext]
        for w, d in [(ATT_W, BF16), (ATT_W, BF16), (LANES, BF16), (MEM_W, F32)]:
            out_shape.append(jax.ShapeDtypeStruct((t, w), d))
            out_specs.append(row(w))
    else:
        for w in (ATT_W, IDX_HEADS * IDX_DIM, MEM_W):
            out_shape.append(jax.ShapeDtypeStruct((t, w), F32))
            out_specs.append(row(w))
    return pl.pallas_call(
        functools.partial(_proj_kernel, prompt=prompt),
        grid=(t // tile,),
        in_specs=in_specs, out_specs=out_specs, out_shape=out_shape,
        compiler_params=_params("parallel"),
    )(*ins)


def _sortable(x):
    b = pltpu.bitcast(jnp.where(x == 0.0, 0.0, x), I32)
    return jnp.where(b < 0, b ^ 0x7FFFFFFF, b)


def _kth_largest_key(count_ge, n_valid, k):
    c0 = count_ge(jnp.zeros(n_valid.shape, I32))
    pos = c0 >= k
    ans, cnt = jnp.where(pos, 0, INT_MIN).astype(I32), jnp.where(pos, c0, n_valid)

    def pending(cnt):
        return jnp.max(jnp.where((cnt != k) & (n_valid >= k), 1.0, 0.0))

    def body(state):
        t, _, ans, cnt = state
        cand = ans | jnp.left_shift(jnp.int32(1), 30 - t)
        c = count_ge(cand)
        up = c >= k
        ans, cnt = jnp.where(up, cand, ans), jnp.where(up, c, cnt)
        return t + 1, pending(cnt), ans, cnt

    state = (jnp.int32(0), pending(cnt), ans, cnt)
    return lax.while_loop(lambda s: (s[0] < 31) & (s[1] > 0.0), body, state)[2]


def _attn_prompt_kernel(qi_ref, q_ref, wi_ref, kid_ref, k_ref, v_ref, o_ref,
                        s_ref, m_ref, l_ref, acc_ref, bias_ref, qk_ref, p_ref, *, tq, tk, topk):
    q0 = pl.program_id(0) * tq
    nkc = (q0 + tq + tk - 1) // tk
    row = q0 + lax.broadcasted_iota(I32, (tq, tk), 0)
    col0 = lax.broadcasted_iota(I32, (tq, tk), 1)
    wi = wi_ref[...]

    def score_chunk(kc, carry):
        off = pl.multiple_of(kc * tk, tk)
        kd = kid_ref[pl.ds(off, tk), :]
        acc = jnp.zeros((tq, tk), F32)
        for h in range(IDX_HEADS):
            acc = acc + wi[:, h:h + 1] * jnp.maximum(_dot_nt(qi_ref[h], kd), 0.0)
        s_ref[kc] = jnp.where(col0 + off <= row, _sortable(acc), INT_MIN)
        return carry

    lax.fori_loop(0, nkc, score_chunk, 0)

    def count_ge(cand):
        candb = jnp.broadcast_to(cand, (tq, LANES))

        def body(kc, cnt):
            blk = s_ref[kc]
            for j in range(tk // LANES):
                cnt = cnt + jnp.where(blk[:, j * LANES:(j + 1) * LANES] >= candb, 1.0, 0.0)
            return cnt

        cnt = lax.fori_loop(0, nkc, body, jnp.zeros((tq, LANES), F32))
        return jnp.sum(cnt, axis=1, keepdims=True)

    n_valid = (q0 + 1 + lax.broadcasted_iota(I32, (tq, 1), 0)).astype(F32)
    thr = jnp.maximum(_kth_largest_key(count_ge, n_valid, float(topk)), INT_MIN + 1)

    m_ref[...] = jnp.full(m_ref.shape, RUNNING_MAX_INIT, F32)
    l_ref[...] = jnp.zeros(l_ref.shape, F32)
    acc_ref[...] = jnp.zeros(acc_ref.shape, F32)

    cols = [slice(j * LANES, (j + 1) * LANES) for j in range(tk // LANES)]

    def attend_chunk(kc, carry):
        off = pl.multiple_of(kc * tk, tk)
        bias_ref[...] = jnp.where(s_ref[kc] >= thr, 0.0, MASKED_SCORE)
        for h in range(ATT_HEADS):
            pair = slice((h // 2) * LANES, (h // 2 + 1) * LANES)
            qk_ref[h] = _dot_nt(q_ref[h], k_ref[pl.ds(off, tk), pair]) + bias_ref[...]
        for h in range(ATT_HEADS):
            mx = functools.reduce(jnp.maximum, [qk_ref[h, :, c] for c in cols])
            m_old = m_ref[h]
            m_new = jnp.maximum(m_old, jnp.max(mx, axis=1, keepdims=True))
            alpha = jnp.exp(m_old - m_new)
            psum = jnp.zeros((tq, LANES), F32)
            for c in cols:
                p = jnp.exp(qk_ref[h, :, c] - m_new)
                psum = psum + p
                p_ref[h, :, c] = p.astype(BF16)
            m_ref[h] = m_new
            l_ref[h] = alpha * l_ref[h] + psum
            acc_ref[h] = alpha * acc_ref[h]
        for h in range(ATT_HEADS):
            pair = slice((h // 2) * LANES, (h // 2 + 1) * LANES)
            acc_ref[h] += _dot(p_ref[h], v_ref[pl.ds(off, tk), pair])
        return carry

    lax.fori_loop(0, nkc, attend_chunk, 0)

    lane = lax.broadcasted_iota(I32, (tq, LANES), 1)
    for j in range(ATT_HEADS // 2):
        even = acc_ref[2 * j] / jnp.sum(l_ref[2 * j], axis=1, keepdims=True)
        odd = acc_ref[2 * j + 1] / jnp.sum(l_ref[2 * j + 1], axis=1, keepdims=True)
        o_ref[:, j * LANES:(j + 1) * LANES] = jnp.where(lane < HEAD_DIM, even, odd)


def attn_prompt(qi_ext, q_ext, wi, ki_dup, k_bf, v_bf, tq=128, tk=512):
    t = k_bf.shape[0]
    tq, tk = min(tq, t), min(tk, t)
    topk = min(IDX_TOPK_MAX, t // 4)
    ext = pl.BlockSpec((ATT_HEADS, tq, LANES), lambda i: (0, i, 0))
    return pl.pallas_call(
        functools.partial(_attn_prompt_kernel, tq=tq, tk=tk, topk=topk),
        grid=(t // tq,),
        in_specs=[ext, ext, pl.BlockSpec((tq, LANES), lambda i: (i, 0)),
                  _full(ki_dup.shape), _full(k_bf.shape), _full(v_bf.shape)],
        out_specs=pl.BlockSpec((tq, ATT_W), lambda i: (i, 0)),
        out_shape=jax.ShapeDtypeStruct((t, ATT_W), F32),
        scratch_shapes=[pltpu.VMEM((t // tk, tq, tk), I32),
                        pltpu.VMEM((ATT_HEADS, tq, LANES), F32),
                        pltpu.VMEM((ATT_HEADS, tq, LANES), F32),
                        pltpu.VMEM((ATT_HEADS, tq, LANES), F32),
                        pltpu.VMEM((tq, tk), F32),
                        pltpu.VMEM((ATT_HEADS, tq, tk), F32),
                        pltpu.VMEM((ATT_HEADS, tq, tk), BF16)],
        compiler_params=_params("arbitrary"),
    )(qi_ext, q_ext, wi, ki_dup, k_bf, v_bf)


SSM_SEG = 32
SSM_CHUNK = SUBLANES * SSM_SEG


def _cexp(re, im):
    mag = jnp.exp(re)
    return mag * jnp.cos(im), mag * jnp.sin(im)


def _ssm_prep_kernel(are_ref, aim_ref, ldt_ref, are16_ref, aim16_ref, bre_ref, bim_ref,
                     bbre_ref, bbim_ref, pre_ref, pim_ref):
    dt = jnp.exp(ldt_ref[...])
    ar, ai = are16_ref[...], aim16_ref[...]
    er, ei = _cexp(ar * dt, ai * dt)
    nr, ni = er - 1.0, ei
    den = ar * ar + ai * ai
    cr, ci = (nr * ar + ni * ai) / den, (ni * ar - nr * ai) / den
    br, bi = bre_ref[...], bim_ref[...]
    bbre_ref[...] = cr * br - ci * bi
    bbim_ref[...] = cr * bi + ci * br
    ar, ai = are_ref[...] * dt, aim_ref[...] * dt
    for j in range(SSM_SEG):
        pr, pi = _cexp(ar * (j + 1.0), ai * (j + 1.0))
        pre_ref[j] = pr
        pim_ref[j] = pi


def ssm_tables(a_re, a_im, b_re, b_im, c_re, c_im, log_dt):
    g, p, c = SSM_GROUPS, SSM_STATE, SSM_GROUP
    rep = lambda a: jnp.repeat(a, c, axis=1)
    out_shape = ([jax.ShapeDtypeStruct((g, p * c), F32)] * 2
                 + [jax.ShapeDtypeStruct((SSM_SEG, g, p), F32)] * 2)
    bbre, bbim, pre, pim = pl.pallas_call(
        _ssm_prep_kernel, out_shape=out_shape,
        compiler_params=pltpu.CompilerParams(vmem_limit_bytes=VMEM_LIMIT),
    )(a_re, a_im, log_dt.reshape(g, 1), rep(a_re), rep(a_im), b_re.reshape(g, p * c), b_im.reshape(g, p * c))
    eye = jnp.eye(g, dtype=F32)
    blockdiag_in = lambda bb: jnp.einsum('gpc,gh->gchp', bb.reshape(g, p, c), eye).reshape(g * c, g * p)
    blockdiag_out = lambda cc: jnp.einsum('gcp,gh->gphc', cc, eye).reshape(g * p, g * c)
    bmat = jnp.concatenate([blockdiag_in(bbre), blockdiag_in(bbim)], axis=1)
    cmat = jnp.concatenate([blockdiag_out(c_re), blockdiag_out(-c_im)], axis=0)
    return bmat, cmat, pre.reshape(SSM_SEG, g * p), pim.reshape(SSM_SEG, g * p)


def _ssm_out(h_all, u, cmat_ref, d_ref, wglu_ref):
    y = _dot(h_all.astype(BF16), cmat_ref[...]) + d_ref[...] * u
    z = jax.nn.gelu(y)
    return z * jax.nn.sigmoid(_dot(z.astype(BF16), wglu_ref[...]))


def _ssm_prompt_kernel(u_ref, bmat_ref, cmat_ref, pre_ref, pim_ref, d_ref, wglu_ref,
                       o_ref, hre_ref, him_ref, h_ref, carry_ref):
    n = SSM_LANES

    @pl.when(pl.program_id(0) == 0)
    def _():
        carry_ref[...] = jnp.zeros(carry_ref.shape, F32)

    u = u_ref[...]
    bu = _dot(u.astype(BF16), bmat_ref[...])
    n_blk = n // LANES
    for b in range(2 * n_blk):
        h_ref[b] = bu[:, b * LANES:(b + 1) * LANES]

    group = 4
    for b0 in range(0, n_blk, group):
        a = [(jnp.broadcast_to(pre_ref[0:1, pl.ds(b * LANES, LANES)], (SUBLANES, LANES)),
              jnp.broadcast_to(pim_ref[0:1, pl.ds(b * LANES, LANES)], (SUBLANES, LANES)))
             for b in range(b0, b0 + group)]

        def step(j, carry):
            rows = pl.ds(j, SUBLANES, stride=SSM_SEG)
            out = []
            for g in range(group):
                (a_re, a_im), (h_re, h_im) = a[g], carry[g]
                n_re = a_re * h_re - a_im * h_im + h_ref[b0 + g, rows, :]
                n_im = a_re * h_im + a_im * h_re + h_ref[n_blk + b0 + g, rows, :]
                h_ref[b0 + g, rows, :] = n_re
                h_ref[n_blk + b0 + g, rows, :] = n_im
                out.append((n_re, n_im))
            return tuple(out)

        zero = jnp.zeros((SUBLANES, LANES), F32)
        lax.fori_loop(0, SSM_SEG, step, ((zero, zero),) * group)

    for b in range(n_blk):
        re_sl, im_sl = pl.ds(b * LANES, LANES), pl.ds(n + b * LANES, LANES)
        p_re, p_im = pre_ref[:, re_sl], pim_ref[:, re_sl]
        f_re, f_im = carry_ref[:, re_sl], carry_ref[:, im_sl]
        for s in range(SUBLANES):
            rows = pl.ds(s * SSM_SEG, SSM_SEG)
            n_re = h_ref[b, rows, :] + p_re * f_re - p_im * f_im
            n_im = h_ref[n_blk + b, rows, :] + p_re * f_im + p_im * f_re
            h_ref[b, rows, :] = n_re
            h_ref[n_blk + b, rows, :] = n_im
            f_re, f_im = n_re[SSM_SEG - 1:, :], n_im[SSM_SEG - 1:, :]
        carry_ref[:, re_sl] = f_re
        carry_ref[:, im_sl] = f_im

    h_all = jnp.concatenate([h_ref[b] for b in range(2 * n_blk)], axis=1)
    o_ref[...] = _ssm_out(h_all, u, cmat_ref, d_ref, wglu_ref)
    hre_ref[...] = carry_ref[:, :n]
    him_ref[...] = carry_ref[:, n:]


def ssm_prompt(u, tables, d, w_glu):
    t = u.shape[0]
    bmat, cmat, pre, pim = tables
    n = SSM_LANES
    ins = [u, bmat.astype(BF16), cmat.astype(BF16), pre, pim, d.reshape(1, -1), w_glu.astype(BF16)]
    return pl.pallas_call(
        _ssm_prompt_kernel,
        grid=(t // SSM_CHUNK,),
        in_specs=[pl.BlockSpec((SSM_CHUNK, SSM_W), lambda i: (i, 0))] + [_full(a.shape) for a in ins[1:]],
        out_specs=[pl.BlockSpec((SSM_CHUNK, SSM_W), lambda i: (i, 0)),
                   pl.BlockSpec((1, n), lambda i: (0, 0)), pl.BlockSpec((1, n), lambda i: (0, 0))],
        out_shape=[jax.ShapeDtypeStruct((t, SSM_W), F32),
                   jax.ShapeDtypeStruct((1, n), F32), jax.ShapeDtypeStruct((1, n), F32)],
        scratch_shapes=[pltpu.VMEM((2 * n // LANES, SSM_CHUNK, LANES), F32), pltpu.VMEM((1, 2 * n), F32)],
        compiler_params=_params("arbitrary"),
    )(*ins)


def _ssm_sample_kernel(u_ref, h0re_ref, h0im_ref, bmat_ref, cmat_ref, pre_ref, pim_ref, d_ref, wglu_ref,
                       o_ref, hre_ref, him_ref):
    n = SSM_LANES
    u = u_ref[...]
    bu = jnp.dot(u, bmat_ref[...], preferred_element_type=F32, precision=lax.Precision.HIGHEST)
    a_re, a_im = pre_ref[0:1, :], pim_ref[0:1, :]
    h0_re, h0_im = h0re_ref[...], h0im_ref[...]
    h_re = bu[:, :n] + a_re * h0_re - a_im * h0_im
    h_im = bu[:, n:] + a_re * h0_im + a_im * h0_re
    hre_ref[...] = h_re
    him_ref[...] = h_im
    o_ref[...] = _ssm_out(jnp.concatenate([h_re, h_im], axis=1), u, cmat_ref, d_ref, wglu_ref)


def ssm_sample(u, h0_re, h0_im, tables, d, w_glu):
    b = u.shape[0]
    bmat, cmat, pre, pim = tables
    n = SSM_LANES
    return pl.pallas_call(
        _ssm_sample_kernel,
        out_shape=[jax.ShapeDtypeStruct((b, SSM_W), F32),
                   jax.ShapeDtypeStruct((b, n), F32), jax.ShapeDtypeStruct((b, n), F32)],
        compiler_params=pltpu.CompilerParams(vmem_limit_bytes=VMEM_LIMIT),
    )(u, h0_re.reshape(b, n), h0_im.reshape(b, n), bmat, cmat.astype(BF16), pre, pim,
      d.reshape(1, -1), w_glu.astype(BF16))


def _merge_kernel(x_ref, gate_ref, att_ref, ssm_ref, mem_ref, wa_ref, ws_ref, wm_ref, wo_ref,
                  nf_ref, wq_ref, skt_ref, h_out, hnt_out, st_out):
    g = gate_ref[...]
    mixed = (g[:, :D_MODEL] * _dot(att_ref[...].astype(BF16), wa_ref[...])
             + g[:, D_MODEL:2 * D_MODEL] * _dot(ssm_ref[...].astype(BF16), ws_ref[...])
             + g[:, 2 * D_MODEL:] * _dot(mem_ref[...].astype(BF16), wm_ref[...]))
    h = x_ref[...] + _dot(mixed.astype(BF16), wo_ref[...])
    hn = _rms(h, nf_ref[...], D_MODEL)
    pq = _dot(hn.astype(BF16), wq_ref[...]).astype(BF16)
    h_out[...] = h
    hnt_out[...] = hn.T.astype(BF16)
    st_out[...] = _dot_nt(skt_ref[...], pq)


def _sub_key_matrix(sub_keys):
    nb = PEER_HEADS * 2
    sk = sub_keys.reshape(nb, N_KEYS, PEER_DK // 2)
    eye = jnp.eye(nb, dtype=sk.dtype)
    return jnp.einsum('bnd,bc->bncd', sk, eye).reshape(nb * N_KEYS, nb * (PEER_DK // 2))


def merge(x, gate, att_o, ssm_o, mem_o, w_br_attn, w_br_ssm, w_br_mem, w_out, norm_ffn, peer_w_q,
          sub_keys, tile=256):
    t = x.shape[0]
    ws = [w.astype(BF16) for w in (w_br_attn, w_br_ssm, w_br_mem, w_out)]
    ins = [x, gate, att_o, ssm_o, mem_o, *ws, norm_ffn.reshape(1, -1), peer_w_q.astype(BF16),
           _sub_key_matrix(sub_keys).astype(BF16)]
    row = lambda w: pl.BlockSpec((tile, w), lambda i: (i, 0))
    n_scores = PEER_HEADS * 2 * N_KEYS
    return pl.pallas_call(
        _merge_kernel,
        grid=(t // tile,),
        in_specs=[row(D_MODEL), row(N_BRANCH * D_MODEL), row(ATT_W), row(SSM_W), row(MEM_W)]
        + [_full(a.shape) for a in ins[5:]],
        out_specs=[row(D_MODEL), pl.BlockSpec((D_MODEL, tile), lambda i: (0, i)),
                   pl.BlockSpec((n_scores, tile), lambda i: (0, i))],
        out_shape=[jax.ShapeDtypeStruct((t, D_MODEL), F32), jax.ShapeDtypeStruct((D_MODEL, t), BF16),
                   jax.ShapeDtypeStruct((n_scores, t), F32)],
        compiler_params=_params("parallel"),
    )(*ins)


PEER_TOKENS = 256
PEER_CHUNK = 2048
NEG_INF = float("-inf")


def _top_values(x, k):
    vals = []
    for _ in range(k):
        m = jnp.max(x, axis=0, keepdims=True)
        vals.append(m)
        x = jnp.where(x == m, NEG_INF, x)
    return vals


def _peer_route(st_ref, cut_ref, s2_ref, e2_ref, f1_ref):
    tokens = st_ref.shape[1]
    for h in range(PEER_HEADS):
        s1 = st_ref[pl.ds(2 * h * N_KEYS, N_KEYS), :]
        s2 = st_ref[pl.ds((2 * h + 1) * N_KEYS, N_KEYS), :]
        a = _top_values(s1, PEER_TOPK)
        b = _top_values(s2, PEER_TOPK)
        cand = [a[i] + b[j] for i in range(PEER_TOPK) for j in range(PEER_TOPK) if (i + 1) * (j + 1) <= PEER_TOPK]
        pad = -len(cand) % SUBLANES
        cand = jnp.concatenate(cand + [jnp.full((pad, tokens), NEG_INF, F32)], axis=0)
        thr = _top_values(cand, PEER_TOPK)[-1]
        top = a[0] + b[0]
        z = jnp.sum(jnp.where(cand >= thr, jnp.exp(cand - top), 0.0), axis=0, keepdims=True)
        cut = jnp.full(s1.shape, jnp.inf, F32)
        for bl in b:
            cut = jnp.where(s1 + bl >= thr, bl, cut)
        cut_ref[h] = cut
        s2_ref[h] = s2
        e2_ref[h] = jnp.exp(s2 - b[0])
        f1_ref[h] = jnp.exp(s1 - a[0]) / z


def _peer_kernel(hnt_ref, st_ref, h_ref, u_ref, vt_ref, y_ref,
                 cut_ref, s2_ref, e2_ref, f1_ref, acc_ref, pt_ref):
    e = pl.program_id(1)
    tokens = hnt_ref.shape[1]
    rows_per_step = PEER_CHUNK // N_KEYS

    @pl.when(e == 0)
    def _():
        _peer_route(st_ref, cut_ref, s2_ref, e2_ref, f1_ref)
        acc_ref[...] = jnp.zeros(acc_ref.shape, F32)

    for il in range(rows_per_step):
        i = e * rows_per_step + il
        sl = slice(il * N_KEYS, (il + 1) * N_KEYS)
        w = jnp.zeros((N_KEYS, tokens), F32)
        for h in range(PEER_HEADS):
            val = e2_ref[h] * f1_ref[h, pl.ds(i, 1), :]
            w = w + jnp.where(s2_ref[h] >= cut_ref[h, pl.ds(i, 1), :], val, 0.0)
        act = jax.nn.gelu(_dot(u_ref[sl, :], hnt_ref[...]))
        pt_ref[sl, :] = (w * act).astype(BF16)
    acc_ref[...] += _dot(vt_ref[...], pt_ref[...])

    @pl.when(e == pl.num_programs(1) - 1)
    def _():
        y_ref[...] = h_ref[...] + acc_ref[...].T


def peer(hnt, st, h, u_bf, vt_bf):
    t = hnt.shape[1]
    tb = PEER_TOKENS
    n_scores = st.shape[0]
    return pl.pallas_call(
        _peer_kernel,
        grid=(t // tb, N_EXPERTS // PEER_CHUNK),
        in_specs=[pl.BlockSpec((D_MODEL, tb), lambda i, e: (0, i)),
                  pl.BlockSpec((n_scores, tb), lambda i, e: (0, i)),
                  pl.BlockSpec((tb, D_MODEL), lambda i, e: (i, 0)),
                  pl.BlockSpec((PEER_CHUNK, D_MODEL), lambda i, e: (e, 0)),
                  pl.BlockSpec((D_MODEL, PEER_CHUNK), lambda i, e: (0, e))],
        out_specs=pl.BlockSpec((tb, D_MODEL), lambda i, e: (i, 0)),
        out_shape=jax.ShapeDtypeStruct((t, D_MODEL), F32),
        scratch_shapes=[pltpu.VMEM((PEER_HEADS, N_KEYS, tb), F32)] * 4
        + [pltpu.VMEM((D_MODEL, tb), F32), pltpu.VMEM((PEER_CHUNK, tb), BF16)],
        compiler_params=_params("parallel", "arbitrary"),
    )(hnt, st, h, u_bf, vt_bf)


def _bf16_round(x):
    return x.astype(BF16).astype(F32)


def _attend_rows(q, kb, vb):
    s = jnp.sum(_bf16_round(kb) * _bf16_round(q)[None], axis=2, keepdims=True)
    p = jnp.exp(s - jnp.max(s, axis=0, keepdims=True))
    p = p / jnp.sum(p, axis=0, keepdims=True)
    return jnp.sum(_bf16_round(p) * _bf16_round(vb), axis=0)


def _mem_sample_kernel(qm_ref, mk_ref, mv_ref, o_ref):
    o_ref[0] = _attend_rows(qm_ref[0] * (MEM_HD ** -0.5), mk_ref[0], mv_ref[0])


def mem_attend_sample(qm, cache_mem_k, cache_mem_v):
    b = qm.shape[0]
    blk = pl.BlockSpec((1, MEM_HEADS, MEM_HD), lambda i: (i, 0, 0))
    cache = pl.BlockSpec((1, MEM_TOKENS, MEM_HEADS, MEM_HD), lambda i: (i, 0, 0, 0))
    return pl.pallas_call(
        _mem_sample_kernel, grid=(b,), in_specs=[blk, cache, cache], out_specs=blk,
        out_shape=jax.ShapeDtypeStruct((b, MEM_HEADS, MEM_HD), F32),
        compiler_params=_params("parallel"),
    )(qm.reshape(b, MEM_HEADS, MEM_HD), cache_mem_k, cache_mem_v)


PAGE_GROUP = 8
RING_SLOTS = 3
SCORE_ROWS = 256


def _select_tile(keys, topk):
    rows = keys.shape[0]

    def count_ge(cand):
        return jnp.sum(jnp.sum(jnp.where(keys >= cand, 1.0, 0.0), axis=0, keepdims=True), axis=1, keepdims=True)

    n_valid = count_ge(jnp.full((1, 1), INT_MIN + 1, I32))
    thr = jnp.maximum(_kth_largest_key(count_ge, n_valid, float(topk)), INT_MIN + 1)
    need = float(topk) - count_ge(thr + 1)
    tie = jnp.where(keys == thr, 1.0, 0.0)
    ri = lax.broadcasted_iota(I32, (LANES, LANES), 0)
    ci = lax.broadcasted_iota(I32, (LANES, LANES), 1)
    within = _dot(tie.astype(BF16), jnp.where(ri <= ci, 1.0, 0.0).astype(BF16))
    row_tot = jnp.broadcast_to(jnp.sum(tie, axis=1, keepdims=True), (rows, LANES))
    rr = lax.broadcasted_iota(I32, (rows, rows), 0)
    rc = lax.broadcasted_iota(I32, (rows, rows), 1)
    before = _dot(jnp.where(rc < rr, 1.0, 0.0).astype(BF16), row_tot.astype(BF16))
    take_tie = jnp.where(within + before <= need, tie, 0.0)
    return jnp.where(keys > thr, 1.0, take_tie)


def _sample_attn_kernel(pt_ref, q_ref, qi_ref, wi_ref, kin_ref, kn_ref, vn_ref, cki_ref, ck_ref, cv_ref,
                        o_ref, kibuf, ring, qb_ref, qib_ref, key_ref, bias_ref, satt_ref, acc_ref,
                        sem_ki, sem_ring, *, topk):
    b, nb = pl.program_id(0), pl.num_programs(0)
    n_pages = pt_ref.shape[1]
    groups = n_pages // PAGE_GROUP
    n_stream = nb * 2 * groups
    par = b % 2

    def ki_copy(bb, buf, j):
        return pltpu.make_async_copy(cki_ref.at[pt_ref[bb, j]], kibuf.at[buf, j], sem_ki.at[buf])

    def start_ki(bb, buf):
        lax.fori_loop(0, n_pages, lambda j, c: (ki_copy(bb, buf, j).start(), c)[1], 0)

    def group_copy(cache_ref, bb, g, slot, i):
        return pltpu.make_async_copy(cache_ref.at[pt_ref[bb, g * PAGE_GROUP + i]], ring.at[slot, i],
                                     sem_ring.at[slot])

    def start_group(n):
        bb, g, slot = n // (2 * groups), n % (2 * groups), n % RING_SLOTS

        @pl.when(g < groups)
        def _():
            for i in range(PAGE_GROUP):
                group_copy(ck_ref, bb, g, slot, i).start()

        @pl.when(g >= groups)
        def _():
            for i in range(PAGE_GROUP):
                group_copy(cv_ref, bb, g - groups, slot, i).start()

    def next_group(n):
        slot = n % RING_SLOTS
        for i in range(PAGE_GROUP):
            group_copy(ck_ref, 0, 0, slot, i).wait()

        @pl.when(n + RING_SLOTS - 1 < n_stream)
        def _():
            start_group(n + RING_SLOTS - 1)

        return slot

    @pl.when(b == 0)
    def _():
        start_ki(0, 0)
        for n in range(RING_SLOTS - 1):
            start_group(n)

    @pl.when(b + 1 < nb)
    def _():
        start_ki(b + 1, 1 - par)

    lax.fori_loop(0, n_pages, lambda j, c: (ki_copy(b, par, j).wait(), c)[1], 0)

    lanes = (IDX_DIM, LANES)
    for h in range(ATT_HEADS):
        qb_ref[h] = jnp.broadcast_to(q_ref[0, h], lanes)
        qib_ref[h] = jnp.broadcast_to(qi_ref[0, h], lanes)
    w = wi_ref[0]
    lane = lax.broadcasted_iota(I32, (1, LANES), 1)

    def head_dots(page_of_head, qsrc):
        return jnp.concatenate([jnp.sum(page_of_head(h) * qsrc[h], axis=0, keepdims=True)
                                for h in range(ATT_HEADS)], axis=0)

    def index_row(page):
        s = head_dots(lambda h: page, qib_ref)
        return _sortable(jnp.sum(w * jnp.maximum(s, 0.0), axis=0, keepdims=True))

    key_ref[...] = jnp.full(key_ref.shape, INT_MIN, I32)

    def score_page(j, c):
        key_ref[pl.ds(j, 1), :] = index_row(kibuf[par, j])
        return c

    lax.fori_loop(0, n_pages, score_page, 0)
    new_row = index_row(jnp.broadcast_to(kin_ref[0], lanes))
    key_ref[pl.ds(n_pages, 1), :] = jnp.where(lane == 0, new_row, INT_MIN)
    bias_ref[...] = jnp.where(_select_tile(key_ref[...], topk) > 0.0, 0.0, MASKED_SCORE)

    def att_row(page, j):
        satt_ref[j] = head_dots(lambda h: page[h], qb_ref) + bias_ref[pl.ds(j, 1), :]

    def k_group(g, c):
        slot = next_group(b * 2 * groups + g)
        for i in range(PAGE_GROUP):
            att_row(ring[slot, i], g * PAGE_GROUP + i)
        return c

    lax.fori_loop(0, groups, k_group, 0)
    k_new = [jnp.broadcast_to(kn_ref[0, h], lanes) for h in range(ATT_HEADS)]
    att_row(k_new, n_pages)

    m = lax.fori_loop(0, n_pages + 1, lambda j, m: jnp.maximum(m, satt_ref[j]),
                      jnp.full((ATT_HEADS, LANES), MASKED_SCORE, F32))
    m = jnp.max(m, axis=1, keepdims=True)

    acc_ref[...] = jnp.zeros(acc_ref.shape, F32)

    def add_page(page, j, psum):
        p = jnp.exp(satt_ref[j] - m)
        for h in range(ATT_HEADS):
            acc_ref[h] += page[h] * p[h:h + 1, :]
        return psum + p

    def v_group(g, psum):
        slot = next_group(b * 2 * groups + groups + g)
        for i in range(PAGE_GROUP):
            psum = add_page(ring[slot, i], g * PAGE_GROUP + i, psum)
        return psum

    psum = lax.fori_loop(0, groups, v_group, jnp.zeros((ATT_HEADS, LANES), F32))
    v_new = [jnp.broadcast_to(vn_ref[0, h], lanes) for h in range(ATT_HEADS)]
    psum = add_page(v_new, n_pages, psum)
    denom = jnp.sum(psum, axis=1, keepdims=True)
    for h in range(ATT_HEADS):
        o_ref[0, h] = jnp.sum(acc_ref[h], axis=1, keepdims=True) / denom[h:h + 1, :]


def sample_attention(q, qi, wi, ki_new, k_new, v_new, cache_idx_k, cache_k, cache_v, page_table, topk):
    b = q.shape[0]
    n_pages = page_table.shape[1]
    col = lambda a, h: a.reshape(b, h, -1, 1)
    cki = jnp.transpose(cache_idx_k, (0, 2, 1))
    ck = jnp.transpose(cache_k, (0, 2, 3, 1))
    cv = jnp.transpose(cache_v, (0, 2, 3, 1))
    head_col = pl.BlockSpec((1, ATT_HEADS, HEAD_DIM, 1), lambda i, pt: (i, 0, 0, 0))
    any_spec = pl.BlockSpec(memory_space=pl.ANY)
    grid_spec = pltpu.PrefetchScalarGridSpec(
        num_scalar_prefetch=1, grid=(b,),
        in_specs=[head_col, head_col,
                  pl.BlockSpec((1, IDX_HEADS, 1), lambda i, pt: (i, 0, 0)),
                  pl.BlockSpec((1, IDX_DIM, 1), lambda i, pt: (i, 0, 0)),
                  head_col, head_col, any_spec, any_spec, any_spec],
        out_specs=head_col,
        scratch_shapes=[pltpu.VMEM((2, n_pages, IDX_DIM, PAGE_SIZE), F32),
                        pltpu.VMEM((RING_SLOTS, PAGE_GROUP, ATT_HEADS, HEAD_DIM, PAGE_SIZE), F32),
                        pltpu.VMEM((ATT_HEADS, HEAD_DIM, LANES), F32),
                        pltpu.VMEM((IDX_HEADS, IDX_DIM, LANES), F32),
                        pltpu.VMEM((SCORE_ROWS, LANES), I32),
                        pltpu.VMEM((SCORE_ROWS, LANES), F32),
                        pltpu.VMEM((n_pages + 1, ATT_HEADS, LANES), F32),
                        pltpu.VMEM((ATT_HEADS, HEAD_DIM, LANES), F32),
                        pltpu.SemaphoreType.DMA((2,)), pltpu.SemaphoreType.DMA((RING_SLOTS,))])
    out = pl.pallas_call(
        functools.partial(_sample_attn_kernel, topk=topk), grid_spec=grid_spec,
        out_shape=jax.ShapeDtypeStruct((b, ATT_HEADS, HEAD_DIM, 1), F32),
        compiler_params=_params("arbitrary"),
    )(page_table, col(q, ATT_HEADS), col(qi, IDX_HEADS), wi[:, :IDX_HEADS].reshape(b, IDX_HEADS, 1),
      ki_new.reshape(b, IDX_DIM, 1), col(k_new, ATT_HEADS), col(v_new, ATT_HEADS), cki, ck, cv)
    return out.reshape(b, ATT_W)


def _pad_rows(a, rows):
    return jnp.pad(a, ((0, rows - a.shape[0]), (0, 0)))


def kernel(x_prompt, x_sample, cache_k, cache_v, cache_idx_k, cache_mem_k, cache_mem_v, state_ssm_re, state_ssm_im, page_table, mem_prompt, norm_mix, w_in, q_norm, k_norm, idx_k_norm, mq_norm, ssm_a_re, ssm_a_im, ssm_b_re, ssm_b_im, ssm_c_re, ssm_c_im, ssm_d, ssm_log_dt, w_glu, mem_norm, w_mem_kv, mk_norm, w_br_attn, w_br_ssm, w_br_mem, w_out, norm_ffn, peer_w_q, peer_sub_keys, peer_u, peer_v):
    seq = x_prompt.shape[1]
    bd = x_sample.shape[0]
    w_parts = _split_w_in(w_in)
    tables = ssm_tables(ssm_a_re, ssm_a_im, ssm_b_re, ssm_b_im, ssm_c_re, ssm_c_im, ssm_log_dt)
    u_bf = peer_u.astype(BF16)
    vt_bf = peer_v.T.astype(BF16)
    tail = (w_br_attn, w_br_ssm, w_br_mem, w_out, norm_ffn, peer_w_q, peer_sub_keys)
    norms = (q_norm, k_norm, idx_k_norm, mq_norm)

    mk, mv = mem_kv(mem_prompt[0], mem_norm, w_mem_kv, mk_norm)
    (k_p, v_p, ki_p, wi, u, gate, q_ext, qi_ext, k_bf, v_bf, ki_dup, mem_o) = project(
        x_prompt[0], jnp.arange(seq), norm_mix, w_parts, *norms, mem=(mk, mv))
    att_o = attn_prompt(qi_ext, q_ext, wi, ki_dup, k_bf, v_bf)
    ssm_o, hre_p, him_p = ssm_prompt(u, tables, ssm_d, w_glu)
    h, hn, st = merge(x_prompt[0], gate, att_o, ssm_o, mem_o, *tail)
    y_prompt = peer(hn, st, h, u_bf, vt_bf)

    (k_s, v_s, ki_s, wi, u, gate, q, qi, qm) = project(
        x_sample[:, 0], jnp.full((bd,), PAST_LEN), norm_mix, w_parts, *norms)
    att_o = sample_attention(q, qi, wi, ki_s, k_s, v_s, cache_idx_k, cache_k, cache_v, page_table,
                             min(IDX_TOPK_MAX, (PAST_LEN + 1) // 4))
    ssm_o, hre_s, him_s = ssm_sample(u, state_ssm_re, state_ssm_im, tables, ssm_d, w_glu)
    mem_o = mem_attend_sample(qm, cache_mem_k, cache_mem_v).reshape(bd, MEM_W)
    pad = lambda a: _pad_rows(a, PEER_TOKENS)
    h, hn, st = merge(pad(x_sample[:, 0]), pad(gate), pad(att_o), pad(ssm_o), pad(mem_o), *tail)
    y_sample = peer(hn, st, h, u_bf, vt_bf)[:bd]

    heads = lambda a: a.reshape(a.shape[0], ATT_HEADS, HEAD_DIM)
    state = lambda a, n: a.reshape(n, SSM_GROUPS, SSM_STATE)
    return (y_prompt[None], y_sample[:, None],
            heads(k_p)[None], heads(v_p)[None], ki_p[None],
            mk.reshape(1, MEM_TOKENS, MEM_HEADS, MEM_HD), mv.reshape(1, MEM_TOKENS, MEM_HEADS, MEM_HD),
            state(hre_p, 1), state(him_p, 1),
            heads(k_s)[:, None], heads(v_s)[:, None], ki_s[:, None],
            state(hre_s, bd), state(him_s, bd))
```

```python
import functools
import math

import jax
import jax.numpy as jnp
import numpy as np
from jax import lax
from jax.experimental import pallas as pl
from jax.experimental.pallas import tpu as pltpu

F32 = jnp.float32
BF16 = jnp.bfloat16
I32 = jnp.int32

D_MODEL = 1024
PAST_LEN = 16384
PAGE_SIZE = 128
ATT_HEADS = 8
HEAD_DIM = 64
ATT_W = ATT_HEADS * HEAD_DIM
IDX_HEADS = 8
IDX_DIM = 64
IDX_TOPK_MAX = 256
ROPE_THETA = 500000.0
ROPE_HALF = HEAD_DIM // 4 // 2
SSM_GROUP = 16
SSM_W = 512
SSM_GROUPS = SSM_W // SSM_GROUP
SSM_STATE = 64
SSM_LANES = SSM_GROUPS * SSM_STATE
MEM_TOKENS = 256
MEM_HEADS = 4
MEM_HD = 128
MEM_W = MEM_HEADS * MEM_HD
N_BRANCH = 3
PEER_HEADS = 8
N_KEYS = 128
N_EXPERTS = N_KEYS * N_KEYS
PEER_DK = 128
PEER_TOPK = 16
EPS = 1e-6

LANES = 128
SUBLANES = 8
VMEM_LIMIT = 56 * 1024 * 1024
INT_MIN = -(2 ** 31)
MASKED_SCORE = -1e30
RUNNING_MAX_INIT = -5e29

NT_DIMS = (((1,), (1,)), ((), ()))


def _params(*sem):
    return pltpu.CompilerParams(dimension_semantics=sem, vmem_limit_bytes=VMEM_LIMIT)


def _full(shape):
    nd = len(shape)
    return pl.BlockSpec(shape, lambda *_: (0,) * nd, pipeline_mode=pl.Buffered(1))


def _rms(x, g, n):
    ms = jnp.sum(x * x, axis=-1, keepdims=True) * (1.0 / n)
    return x * lax.rsqrt(ms + EPS) * g


def _dot(a, b):
    return jnp.dot(a, b, preferred_element_type=F32)


def _dot_nt(a, b):
    return lax.dot_general(a, b, NT_DIMS, preferred_element_type=F32)


def _mem_kv_kernel(mem_ref, mn_ref, w_ref, mkn_ref, mk_ref, mv_ref):
    xn = _rms(mem_ref[...], mn_ref[...], D_MODEL).astype(BF16)
    y = _dot(xn, w_ref[...])
    for h in range(MEM_HEADS):
        sl = slice(h * MEM_HD, (h + 1) * MEM_HD)
        mk_ref[:, sl] = _rms(y[:, sl], mkn_ref[...], MEM_HD)
    mv_ref[...] = y[:, MEM_W:]


def mem_kv(mem, mem_norm, w_mem_kv, mk_norm):
    m = mem.shape[0]
    return pl.pallas_call(
        _mem_kv_kernel,
        out_shape=(jax.ShapeDtypeStruct((m, MEM_W), F32), jax.ShapeDtypeStruct((m, MEM_W), F32)),
        compiler_params=pltpu.CompilerParams(vmem_limit_bytes=VMEM_LIMIT),
    )(mem, mem_norm.reshape(1, -1), w_mem_kv.astype(BF16), mk_norm.reshape(1, -1))


def _rope_tables(pos):
    inv_freq = ROPE_THETA ** (-jnp.arange(ROPE_HALF, dtype=F32) / ROPE_HALF)
    ang = pos.astype(F32)[:, None] * inv_freq[None, :]
    cos, sin = jnp.cos(ang), jnp.sin(ang)
    t = pos.shape[0]
    rest = HEAD_DIM - 2 * ROPE_HALF
    c = jnp.concatenate([cos, cos, jnp.ones((t, rest), F32)], axis=1)
    s_hi = jnp.concatenate([-sin, jnp.zeros((t, HEAD_DIM - ROPE_HALF), F32)], axis=1)
    s_lo = jnp.concatenate([jnp.zeros((t, ROPE_HALF), F32), sin, jnp.zeros((t, rest), F32)], axis=1)
    rep = LANES // HEAD_DIM
    return jnp.tile(c, (1, rep)), jnp.tile(s_hi, (1, rep)), jnp.tile(s_lo, (1, rep))


def _rope(y, c, s_hi, s_lo):
    w = y.shape[1]
    rep = w // LANES
    if rep > 1:
        c = jnp.concatenate([c] * rep, axis=1)
        s_hi = jnp.concatenate([s_hi] * rep, axis=1)
        s_lo = jnp.concatenate([s_lo] * rep, axis=1)
    return y * c + pltpu.roll(y, w - ROPE_HALF, axis=1) * s_hi + pltpu.roll(y, ROPE_HALF, axis=1) * s_lo


def _group_rms(y, grp, g, n):
    sq = y * y
    hi = sq.astype(BF16)
    lo = (sq - hi.astype(F32)).astype(BF16)
    ss = _dot(hi, grp) + _dot(lo, grp)
    return y * lax.rsqrt(ss * (1.0 / n) + EPS) * g


def _proj_kernel(*refs, prompt):
    (x_ref, nm_ref, wq_ref, wk_ref, wv_ref, wqi_ref, wki_ref, wwi_ref, wu_ref, wqm_ref, wg_ref,
     qn_ref, kn_ref, ikn_ref, mqn_ref, grp_ref, c_ref, shi_ref, slo_ref) = refs[:19]
    refs = refs[19:]
    if prompt:
        mk_ref, mv_ref = refs[:2]
        (k_out, v_out, ki_out, wi_out, u_out, gate_out,
         qx_out, qix_out, kbf_out, vbf_out, kid_out, memo_out) = refs[2:]
    else:
        k_out, v_out, ki_out, wi_out, u_out, gate_out, q_out, qi_out, qm_out = refs

    xn = _rms(x_ref[...], nm_ref[...], D_MODEL).astype(BF16)
    c, s_hi, s_lo = c_ref[...], shi_ref[...], slo_ref[...]
    grp = grp_ref[...]

    q = _rope(_group_rms(_dot(xn, wq_ref[...]), grp, qn_ref[...], HEAD_DIM), c, s_hi, s_lo)
    q = q * (HEAD_DIM ** -0.5)
    k = _rope(_group_rms(_dot(xn, wk_ref[...]), grp, kn_ref[...], HEAD_DIM), c, s_hi, s_lo)
    v = _dot(xn, wv_ref[...])
    qi = _rope(_dot(xn, wqi_ref[...]), c, s_hi, s_lo)
    ki2 = _dot(xn, wki_ref[...])
    ki2 = _rope(_rms(ki2, ikn_ref[...], 2 * IDX_DIM), c, s_hi, s_lo)
    wi = _dot(xn, wwi_ref[...]) * (IDX_HEADS ** -0.5 * IDX_DIM ** -0.5)
    qm = _dot(xn, wqm_ref[...])
    qm = jnp.concatenate(
        [_rms(qm[:, h * MEM_HD:(h + 1) * MEM_HD], mqn_ref[...], MEM_HD) for h in range(MEM_HEADS)], axis=1)

    k_out[...] = k
    v_out[...] = v
    ki_out[...] = ki2[:, :IDX_DIM]
    wi_out[...] = wi
    u_out[...] = _dot(xn, wu_ref[...])
    gate_out[...] = jax.nn.sigmoid(_dot(xn, wg_ref[...]))

    if not prompt:
        q_out[...] = q
        qi_out[...] = qi
        qm_out[...] = qm
        return

    kbf_out[...] = k.astype(BF16)
    vbf_out[...] = v.astype(BF16)
    kid_out[...] = ki2.astype(BF16)
    lane = lax.broadcasted_iota(I32, (q.shape[0], LANES), 1)
    for h in range(ATT_HEADS):
        pair = slice((h // 2) * LANES, (h // 2 + 1) * LANES)
        own = (lane < HEAD_DIM) if h % 2 == 0 else (lane >= HEAD_DIM)
        qx_out[h] = jnp.where(own, q[:, pair], 0.0).astype(BF16)
        qix_out[h] = jnp.where(own, qi[:, pair], 0.0).astype(BF16)

    for h in range(MEM_HEADS):
        sl = slice(h * MEM_HD, (h + 1) * MEM_HD)
        s = _dot_nt(qm[:, sl].astype(BF16), mk_ref[:, sl]) * (MEM_HD ** -0.5)
        p = jnp.exp(s - jnp.max(s, axis=-1, keepdims=True))
        p = p / jnp.sum(p, axis=-1, keepdims=True)
        memo_out[:, sl] = _dot(p.astype(BF16), mv_ref[:, sl])


def _split_w_in(w_in):
    splits = np.cumsum([ATT_W, ATT_W, ATT_W, IDX_HEADS * IDX_DIM, IDX_DIM, IDX_HEADS, SSM_W, MEM_W])
    wq, wk, wv, wqi, wki, wwi, wu, wqm, wg = jnp.split(w_in.astype(BF16), splits.tolist(), axis=1)
    wki = jnp.concatenate([wki, wki], axis=1)
    wwi = jnp.pad(wwi, ((0, 0), (0, LANES - IDX_HEADS)))
    return wq, wk, wv, wqi, wki, wwi, wu, wqm, wg


def project(x, pos, norm_mix, w_parts, q_norm, k_norm, idx_k_norm, mq_norm, mem=None, tile=256):
    t = x.shape[0]
    tile = min(tile, t)
    prompt = mem is not None
    heads_per_slab = ATT_W // HEAD_DIM
    grp = jnp.kron(jnp.eye(heads_per_slab, dtype=F32), jnp.ones((HEAD_DIM, HEAD_DIM), F32)).astype(BF16)
    c, s_hi, s_lo = _rope_tables(pos)
    tile_rep = lambda g, n: jnp.tile(g.reshape(1, -1), (1, n))
    ins = [x, norm_mix.reshape(1, -1), *w_parts,
           tile_rep(q_norm, ATT_HEADS), tile_rep(k_norm, ATT_HEADS), tile_rep(idx_k_norm, 2),
           mq_norm.reshape(1, -1), grp, c, s_hi, s_lo]
    row = lambda w: pl.BlockSpec((tile, w), lambda i: (i, 0))
    in_specs = [row(D_MODEL)] + [_full(a.shape) for a in ins[1:16]] + [row(LANES)] * 3
    outs = [(ATT_W, F32), (ATT_W, F32), (IDX_DIM, F32), (LANES, F32), (SSM_W, F32), (N_BRANCH * D_MODEL, F32)]
    out_shape = [jax.ShapeDtypeStruct((t, w), d) for w, d in outs]
    out_specs = [row(w) for w, _ in outs]
    if prompt:
        mk, mv = mem
        ins += [mk.astype(BF16), mv.astype(BF16)]
        in_specs += [_full(mk.shape), _full(mv.shape)]
        ext = pl.BlockSpec((ATT_HEADS, tile, LANES), lambda i: (0, i, 0))
        out_shape += [jax.ShapeDtypeStruct((ATT_HEADS, t, LANES), BF16)] * 2
        out_specs += [ext, ext]
        for w, d in [(ATT_W, BF16), (ATT_W, BF16), (LANES, BF16), (MEM_W, F32)]:
            out_shape.append(jax.ShapeDtypeStruct((t, w), d))
            out_specs.append(row(w))
    else:
        for w in (ATT_W, IDX_HEADS * IDX_DIM, MEM_W):
            out_shape.append(jax.ShapeDtypeStruct((t, w), F32))
            out_specs.append(row(w))
    return pl.pallas_call(
        functools.partial(_proj_kernel, prompt=prompt),
        grid=(t // tile,),
        in_specs=in_specs, out_specs=out_specs, out_shape=out_shape,
        compiler_params=_params("parallel"),
    )(*ins)


def _sortable(x):
    b = pltpu.bitcast(jnp.where(x == 0.0, 0.0, x), I32)
    return jnp.where(b < 0, b ^ 0x7FFFFFFF, b)


def _kth_largest_key(count_ge, n_valid, k):
    c0 = count_ge(jnp.zeros(n_valid.shape, I32))
    pos = c0 >= k
    ans, cnt = jnp.where(pos, 0, INT_MIN).astype(I32), jnp.where(pos, c0, n_valid)

    def pending(cnt):
        return jnp.max(jnp.where((cnt != k) & (n_valid >= k), 1.0, 0.0))

    def body(state):
        t, _, ans, cnt = state
        cand = ans | jnp.left_shift(jnp.int32(1), 30 - t)
        c = count_ge(cand)
        up = c >= k
        ans, cnt = jnp.where(up, cand, ans), jnp.where(up, c, cnt)
        return t + 1, pending(cnt), ans, cnt

    state = (jnp.int32(0), pending(cnt), ans, cnt)
    return lax.while_loop(lambda s: (s[0] < 31) & (s[1] > 0.0), body, state)[2]


def _attn_prompt_kernel(qi_ref, q_ref, wi_ref, kid_ref, k_ref, v_ref, o_ref,
                        s_ref, m_ref, l_ref, acc_ref, bias_ref, qk_ref, p_ref, *, tq, tk, topk):
    q0 = pl.program_id(0) * tq
    nkc = (q0 + tq + tk - 1) // tk
    row = q0 + lax.broadcasted_iota(I32, (tq, tk), 0)
    col0 = lax.broadcasted_iota(I32, (tq, tk), 1)
    wi = wi_ref[...]

    def score_chunk(kc, carry):
        off = pl.multiple_of(kc * tk, tk)
        kd = kid_ref[pl.ds(off, tk), :]
        acc = jnp.zeros((tq, tk), F32)
        for h in range(IDX_HEADS):
            acc = acc + wi[:, h:h + 1] * jnp.maximum(_dot_nt(qi_ref[h], kd), 0.0)
        s_ref[kc] = jnp.where(col0 + off <= row, _sortable(acc), INT_MIN)
        return carry

    lax.fori_loop(0, nkc, score_chunk, 0)

    def count_ge(cand):
        candb = jnp.broadcast_to(cand, (tq, LANES))

        def body(kc, cnt):
            blk = s_ref[kc]
            for j in range(tk // LANES):
                cnt = cnt + jnp.where(blk[:, j * LANES:(j + 1) * LANES] >= candb, 1.0, 0.0)
            return cnt

        cnt = lax.fori_loop(0, nkc, body, jnp.zeros((tq, LANES), F32))
        return jnp.sum(cnt, axis=1, keepdims=True)

    n_valid = (q0 + 1 + lax.broadcasted_iota(I32, (tq, 1), 0)).astype(F32)
    thr = jnp.maximum(_kth_largest_key(count_ge, n_valid, float(topk)), INT_MIN + 1)

    m_ref[...] = jnp.full(m_ref.shape, RUNNING_MAX_INIT, F32)
    l_ref[...] = jnp.zeros(l_ref.shape, F32)
    acc_ref[...] = jnp.zeros(acc_ref.shape, F32)

    cols = [slice(j * LANES, (j + 1) * LANES) for j in range(tk // LANES)]

    def attend_chunk(kc, carry):
        off = pl.multiple_of(kc * tk, tk)
        bias_ref[...] = jnp.where(s_ref[kc] >= thr, 0.0, MASKED_SCORE)
        for h in range(ATT_HEADS):
            pair = slice((h // 2) * LANES, (h // 2 + 1) * LANES)
            qk_ref[h] = _dot_nt(q_ref[h], k_ref[pl.ds(off, tk), pair]) + bias_ref[...]
        for h in range(ATT_HEADS):
            mx = functools.reduce(jnp.maximum, [qk_ref[h, :, c] for c in cols])
            m_old = m_ref[h]
            m_new = jnp.maximum(m_old, jnp.max(mx, axis=1, keepdims=True))
            alpha = jnp.exp(m_old - m_new)
            psum = jnp.zeros((tq, LANES), F32)
            for c in cols:
                p = jnp.exp(qk_ref[h, :, c] - m_new)
                psum = psum + p
                p_ref[h, :, c] = p.astype(BF16)
            m_ref[h] = m_new
            l_ref[h] = alpha * l_ref[h] + psum
            acc_ref[h] = alpha * acc_ref[h]
        for h in range(ATT_HEADS):
            pair = slice((h // 2) * LANES, (h // 2 + 1) * LANES)
            acc_ref[h] += _dot(p_ref[h], v_ref[pl.ds(off, tk), pair])
        return carry

    lax.fori_loop(0, nkc, attend_chunk, 0)

    lane = lax.broadcasted_iota(I32, (tq, LANES), 1)
    for j in range(ATT_HEADS // 2):
        even = acc_ref[2 * j] / jnp.sum(l_ref[2 * j], axis=1, keepdims=True)
        odd = acc_ref[2 * j + 1] / jnp.sum(l_ref[2 * j + 1], axis=1, keepdims=True)
        o_ref[:, j * LANES:(j + 1) * LANES] = jnp.where(lane < HEAD_DIM, even, odd)


def attn_prompt(qi_ext, q_ext, wi, ki_dup, k_bf, v_bf, tq=128, tk=512):
    t = k_bf.shape[0]
    tq, tk = min(tq, t), min(tk, t)
    topk = min(IDX_TOPK_MAX, t // 4)
    ext = pl.BlockSpec((ATT_HEADS, tq, LANES), lambda i: (0, i, 0))
    return pl.pallas_call(
        functools.partial(_attn_prompt_kernel, tq=tq, tk=tk, topk=topk),
        grid=(t // tq,),
        in_specs=[ext, ext, pl.BlockSpec((tq, LANES), lambda i: (i, 0)),
                  _full(ki_dup.shape), _full(k_bf.shape), _full(v_bf.shape)],
        out_specs=pl.BlockSpec((tq, ATT_W), lambda i: (i, 0)),
        out_shape=jax.ShapeDtypeStruct((t, ATT_W), F32),
        scratch_shapes=[pltpu.VMEM((t // tk, tq, tk), I32),
                        pltpu.VMEM((ATT_HEADS, tq, LANES), F32),
                        pltpu.VMEM((ATT_HEADS, tq, LANES), F32),
                        pltpu.VMEM((ATT_HEADS, tq, LANES), F32),
                        pltpu.VMEM((tq, tk), F32),
                        pltpu.VMEM((ATT_HEADS, tq, tk), F32),
                        pltpu.VMEM((ATT_HEADS, tq, tk), BF16)],
        compiler_params=_params("arbitrary"),
    )(qi_ext, q_ext, wi, ki_dup, k_bf, v_bf)


SSM_SEG = 32
SSM_CHUNK = SUBLANES * SSM_SEG


def _cexp(re, im):
    mag = jnp.exp(re)
    return mag * jnp.cos(im), mag * jnp.sin(im)


def _ssm_prep_kernel(are_ref, aim_ref, ldt_ref, are16_ref, aim16_ref, bre_ref, bim_ref,
                     bbre_ref, bbim_ref, pre_ref, pim_ref):
    dt = jnp.exp(ldt_ref[...])
    ar, ai = are16_ref[...], aim16_ref[...]
    er, ei = _cexp(ar * dt, ai * dt)
    nr, ni = er - 1.0, ei
    den = ar * ar + ai * ai
    cr, ci = (nr * ar + ni * ai) / den, (ni * ar - nr * ai) / den
    br, bi = bre_ref[...], bim_ref[...]
    bbre_ref[...] = cr * br - ci * bi
    bbim_ref[...] = cr * bi + ci * br
    ar, ai = are_ref[...] * dt, aim_ref[...] * dt
    for j in range(SSM_SEG):
        pr, pi = _cexp(ar * (j + 1.0), ai * (j + 1.0))
        pre_ref[j] = pr
        pim_ref[j] = pi


def ssm_tables(a_re, a_im, b_re, b_im, c_re, c_im, log_dt):
    g, p, c = SSM_GROUPS, SSM_STATE, SSM_GROUP
    rep = lambda a: jnp.repeat(a, c, axis=1)
    out_shape = ([jax.ShapeDtypeStruct((g, p * c), F32)] * 2
                 + [jax.ShapeDtypeStruct((SSM_SEG, g, p), F32)] * 2)
    bbre, bbim, pre, pim = pl.pallas_call(
        _ssm_prep_kernel, out_shape=out_shape,
        compiler_params=pltpu.CompilerParams(vmem_limit_bytes=VMEM_LIMIT),
    )(a_re, a_im, log_dt.reshape(g, 1), rep(a_re), rep(a_im), b_re.reshape(g, p * c), b_im.reshape(g, p * c))
    eye = jnp.eye(g, dtype=F32)
    blockdiag_in = lambda bb: jnp.einsum('gpc,gh->gchp', bb.reshape(g, p, c), eye).reshape(g * c, g * p)
    blockdiag_out = lambda cc: jnp.einsum('gcp,gh->gphc', cc, eye).reshape(g * p, g * c)
    bmat = jnp.concatenate([blockdiag_in(bbre), blockdiag_in(bbim)], axis=1)
    cmat = jnp.concatenate([blockdiag_out(c_re), blockdiag_out(-c_im)], axis=0)
    return bmat, cmat, pre.reshape(SSM_SEG, g * p), pim.reshape(SSM_SEG, g * p)


def _ssm_out(h_all, u, cmat_ref, d_ref, wglu_ref):
    y = _dot(h_all.astype(BF16), cmat_ref[...]) + d_ref[...] * u
    z = jax.nn.gelu(y)
    return z * jax.nn.sigmoid(_dot(z.astype(BF16), wglu_ref[...]))


def _ssm_prompt_kernel(u_ref, bmat_ref, cmat_ref, pre_ref, pim_ref, d_ref, wglu_ref,
                       o_ref, hre_ref, him_ref, bu_ref, h_ref, carry_ref):
    n = SSM_LANES

    @pl.when(pl.program_id(0) == 0)
    def _():
        carry_ref[...] = jnp.zeros(carry_ref.shape, F32)

    u = u_ref[...]
    bu = _dot(u.astype(BF16), bmat_ref[...])
    n_blk = n // LANES
    for b in range(2 * n_blk):
        bu_ref[b] = bu[:, b * LANES:(b + 1) * LANES]

    group = 8
    for b0 in range(0, n_blk, group):
        a = [(jnp.broadcast_to(pre_ref[0:1, pl.ds(b * LANES, LANES)], (SUBLANES, LANES)),
              jnp.broadcast_to(pim_ref[0:1, pl.ds(b * LANES, LANES)], (SUBLANES, LANES)))
             for b in range(b0, b0 + group)]

        def step(j, carry):
            rows = pl.ds(pl.multiple_of(j * SUBLANES, SUBLANES), SUBLANES)
            out = []
            for g in range(group):
                (a_re, a_im), (h_re, h_im) = a[g], carry[g]
                n_re = a_re * h_re - a_im * h_im + bu_ref[b0 + g, rows, :]
                n_im = a_re * h_im + a_im * h_re + bu_ref[n_blk + b0 + g, rows, :]
                h_ref[b0 + g, rows, :] = n_re
                h_ref[n_blk + b0 + g, rows, :] = n_im
                out.append((n_re, n_im))
            return tuple(out)

        zero = jnp.zeros((SUBLANES, LANES), F32)
        lax.fori_loop(0, SSM_SEG, step, ((zero, zero),) * group)

    last = pl.ds((SSM_SEG - 1) * SUBLANES, SUBLANES)
    for b in range(n_blk):
        re_sl, im_sl = pl.ds(b * LANES, LANES), pl.ds(n + b * LANES, LANES)
        e_re, e_im = h_ref[b, last, :], h_ref[n_blk + b, last, :]
        w_re, w_im = pre_ref[SSM_SEG - 1:SSM_SEG, re_sl], pim_ref[SSM_SEG - 1:SSM_SEG, re_sl]
        f_re, f_im = carry_ref[:, re_sl], carry_ref[:, im_sl]
        enter_re, enter_im = [], []
        for s in range(SUBLANES):
            enter_re.append(f_re)
            enter_im.append(f_im)
            f_re, f_im = (e_re[s:s + 1] + w_re * f_re - w_im * f_im,
                          e_im[s:s + 1] + w_re * f_im + w_im * f_re)
        carry_ref[:, re_sl] = f_re
        carry_ref[:, im_sl] = f_im
        g_re, g_im = jnp.concatenate(enter_re, axis=0), jnp.concatenate(enter_im, axis=0)
        for j in range(SSM_SEG):
            rows = pl.ds(j * SUBLANES, SUBLANES)
            p_re, p_im = pre_ref[j:j + 1, re_sl], pim_ref[j:j + 1, re_sl]
            h_ref[b, rows, :] += p_re * g_re - p_im * g_im
            h_ref[n_blk + b, rows, :] += p_re * g_im + p_im * g_re

    h_all = jnp.concatenate([h_ref[b] for b in range(2 * n_blk)], axis=1)
    o_ref[...] = _ssm_out(h_all, u, cmat_ref, d_ref, wglu_ref)
    hre_ref[...] = carry_ref[:, :n]
    him_ref[...] = carry_ref[:, n:]


def ssm_prompt(u, tables, d, w_glu):
    t = u.shape[0]
    bmat, cmat, pre, pim = tables
    n = SSM_LANES
    n_chunks = t // SSM_CHUNK
    to_steps = lambda a: a.reshape(n_chunks, SUBLANES, SSM_SEG, -1).swapaxes(1, 2).reshape(t, -1)
    to_time = lambda a: a.reshape(n_chunks, SSM_SEG, SUBLANES, -1).swapaxes(1, 2).reshape(t, -1)
    ins = [to_steps(u), bmat.astype(BF16), cmat.astype(BF16), pre, pim, d.reshape(1, -1), w_glu.astype(BF16)]
    o, h_re, h_im = pl.pallas_call(
        _ssm_prompt_kernel,
        grid=(t // SSM_CHUNK,),
        in_specs=[pl.BlockSpec((SSM_CHUNK, SSM_W), lambda i: (i, 0))] + [_full(a.shape) for a in ins[1:]],
        out_specs=[pl.BlockSpec((SSM_CHUNK, SSM_W), lambda i: (i, 0)),
                   pl.BlockSpec((1, n), lambda i: (0, 0)), pl.BlockSpec((1, n), lambda i: (0, 0))],
        out_shape=[jax.ShapeDtypeStruct((t, SSM_W), F32),
                   jax.ShapeDtypeStruct((1, n), F32), jax.ShapeDtypeStruct((1, n), F32)],
        scratch_shapes=[pltpu.VMEM((2 * n // LANES, SSM_CHUNK, LANES), F32)] * 2 + [pltpu.VMEM((1, 2 * n), F32)],
        compiler_params=_params("arbitrary"),
    )(*ins)
    return to_time(o), h_re, h_im


def _ssm_sample_kernel(u_ref, h0re_ref, h0im_ref, bmat_ref, cmat_ref, pre_ref, pim_ref, d_ref, wglu_ref,
                       o_ref, hre_ref, him_ref):
    n = SSM_LANES
    u = u_ref[...]
    bu = jnp.dot(u, bmat_ref[...], preferred_element_type=F32, precision=lax.Precision.HIGHEST)
    a_re, a_im = pre_ref[0:1, :], pim_ref[0:1, :]
    h0_re, h0_im = h0re_ref[...], h0im_ref[...]
    h_re = bu[:, :n] + a_re * h0_re - a_im * h0_im
    h_im = bu[:, n:] + a_re * h0_im + a_im * h0_re
    hre_ref[...] = h_re
    him_ref[...] = h_im
    o_ref[...] = _ssm_out(jnp.concatenate([h_re, h_im], axis=1), u, cmat_ref, d_ref, wglu_ref)


def ssm_sample(u, h0_re, h0_im, tables, d, w_glu):
    b = u.shape[0]
    bmat, cmat, pre, pim = tables
    n = SSM_LANES
    return pl.pallas_call(
        _ssm_sample_kernel,
        out_shape=[jax.ShapeDtypeStruct((b, SSM_W), F32),
                   jax.ShapeDtypeStruct((b, n), F32), jax.ShapeDtypeStruct((b, n), F32)],
        compiler_params=pltpu.CompilerParams(vmem_limit_bytes=VMEM_LIMIT),
    )(u, h0_re.reshape(b, n), h0_im.reshape(b, n), bmat, cmat.astype(BF16), pre, pim,
      d.reshape(1, -1), w_glu.astype(BF16))


def _merge_kernel(x_ref, gate_ref, att_ref, ssm_ref, mem_ref, wa_ref, ws_ref, wm_ref, wo_ref,
                  nf_ref, wq_ref, skt_ref, h_out, hnt_out, st_out):
    g = gate_ref[...]
    mixed = (g[:, :D_MODEL] * _dot(att_ref[...].astype(BF16), wa_ref[...])
             + g[:, D_MODEL:2 * D_MODEL] * _dot(ssm_ref[...].astype(BF16), ws_ref[...])
             + g[:, 2 * D_MODEL:] * _dot(mem_ref[...].astype(BF16), wm_ref[...]))
    h = x_ref[...] + _dot(mixed.astype(BF16), wo_ref[...])
    hn = _rms(h, nf_ref[...], D_MODEL)
    pq = _dot(hn.astype(BF16), wq_ref[...]).astype(BF16)
    h_out[...] = h
    hnt_out[...] = hn.T.astype(BF16)
    st_out[...] = _dot_nt(skt_ref[...], pq)


def _sub_key_matrix(sub_keys):
    nb = PEER_HEADS * 2
    sk = sub_keys.reshape(nb, N_KEYS, PEER_DK // 2)
    eye = jnp.eye(nb, dtype=sk.dtype)
    return jnp.einsum('bnd,bc->bncd', sk, eye).reshape(nb * N_KEYS, nb * (PEER_DK // 2))


def merge(x, gate, att_o, ssm_o, mem_o, w_br_attn, w_br_ssm, w_br_mem, w_out, norm_ffn, peer_w_q,
          sub_keys, tile=256):
    t = x.shape[0]
    ws = [w.astype(BF16) for w in (w_br_attn, w_br_ssm, w_br_mem, w_out)]
    ins = [x, gate, att_o, ssm_o, mem_o, *ws, norm_ffn.reshape(1, -1), peer_w_q.astype(BF16),
           _sub_key_matrix(sub_keys).astype(BF16)]
    row = lambda w: pl.BlockSpec((tile, w), lambda i: (i, 0))
    n_scores = PEER_HEADS * 2 * N_KEYS
    return pl.pallas_call(
        _merge_kernel,
        grid=(t // tile,),
        in_specs=[row(D_MODEL), row(N_BRANCH * D_MODEL), row(ATT_W), row(SSM_W), row(MEM_W)]
        + [_full(a.shape) for a in ins[5:]],
        out_specs=[row(D_MODEL), pl.BlockSpec((D_MODEL, tile), lambda i: (0, i)),
                   pl.BlockSpec((n_scores, tile), lambda i: (0, i))],
        out_shape=[jax.ShapeDtypeStruct((t, D_MODEL), F32), jax.ShapeDtypeStruct((D_MODEL, t), BF16),
                   jax.ShapeDtypeStruct((n_scores, t), F32)],
        compiler_params=_params("parallel"),
    )(*ins)


PEER_TOKENS = 256
PEER_CHUNK = 2048
NEG_INF = float("-inf")


def _top_values(x, k):
    vals = []
    for _ in range(k):
        m = jnp.max(x, axis=0, keepdims=True)
        vals.append(m)
        x = jnp.where(x == m, NEG_INF, x)
    return vals


def _peer_route(st_ref, cut_ref, s2_ref, e2_ref, f1_ref):
    tokens = st_ref.shape[1]
    for h in range(PEER_HEADS):
        s1 = st_ref[pl.ds(2 * h * N_KEYS, N_KEYS), :]
        s2 = st_ref[pl.ds((2 * h + 1) * N_KEYS, N_KEYS), :]
        a = _top_values(s1, PEER_TOPK)
        b = _top_values(s2, PEER_TOPK)
        cand = [a[i] + b[j] for i in range(PEER_TOPK) for j in range(PEER_TOPK) if (i + 1) * (j + 1) <= PEER_TOPK]
        pad = -len(cand) % SUBLANES
        cand = jnp.concatenate(cand + [jnp.full((pad, tokens), NEG_INF, F32)], axis=0)
        thr = _top_values(cand, PEER_TOPK)[-1]
        top = a[0] + b[0]
        z = jnp.sum(jnp.where(cand >= thr, jnp.exp(cand - top), 0.0), axis=0, keepdims=True)
        cut = jnp.full(s1.shape, jnp.inf, F32)
        for bl in b:
            cut = jnp.where(s1 + bl >= thr, bl, cut)
        cut_ref[h] = cut
        s2_ref[h] = s2
        e2_ref[h] = jnp.exp(s2 - b[0])
        f1_ref[h] = jnp.exp(s1 - a[0]) / z


def _peer_kernel(hnt_ref, st_ref, h_ref, u_ref, vt_ref, y_ref,
                 cut_ref, s2_ref, e2_ref, f1_ref, acc_ref, pt_ref):
    e = pl.program_id(1)
    tokens = hnt_ref.shape[1]
    rows_per_step = PEER_CHUNK // N_KEYS

    @pl.when(e == 0)
    def _():
        _peer_route(st_ref, cut_ref, s2_ref, e2_ref, f1_ref)
        acc_ref[...] = jnp.zeros(acc_ref.shape, F32)

    for il in range(rows_per_step):
        i = e * rows_per_step + il
        sl = slice(il * N_KEYS, (il + 1) * N_KEYS)
        w = jnp.zeros((N_KEYS, tokens), F32)
        for h in range(PEER_HEADS):
            val = e2_ref[h] * f1_ref[h, pl.ds(i, 1), :]
            w = w + jnp.where(s2_ref[h] >= cut_ref[h, pl.ds(i, 1), :], val, 0.0)
        act = jax.nn.gelu(_dot(u_ref[sl, :], hnt_ref[...]))
        pt_ref[sl, :] = (w * act).astype(BF16)
    acc_ref[...] += _dot(vt_ref[...], pt_ref[...])

    @pl.when(e == pl.num_programs(1) - 1)
    def _():
        y_ref[...] = h_ref[...] + acc_ref[...].T


def peer(hnt, st, h, u_bf, vt_bf):
    t = hnt.shape[1]
    tb = PEER_TOKENS
    n_scores = st.shape[0]
    return pl.pallas_call(
        _peer_kernel,
        grid=(t // tb, N_EXPERTS // PEER_CHUNK),
        in_specs=[pl.BlockSpec((D_MODEL, tb), lambda i, e: (0, i)),
                  pl.BlockSpec((n_scores, tb), lambda i, e: (0, i)),
                  pl.BlockSpec((tb, D_MODEL), lambda i, e: (i, 0)),
                  pl.BlockSpec((PEER_CHUNK, D_MODEL), lambda i, e: (e, 0)),
                  pl.BlockSpec((D_MODEL, PEER_CHUNK), lambda i, e: (0, e))],
        out_specs=pl.BlockSpec((tb, D_MODEL), lambda i, e: (i, 0)),
        out_shape=jax.ShapeDtypeStruct((t, D_MODEL), F32),
        scratch_shapes=[pltpu.VMEM((PEER_HEADS, N_KEYS, tb), F32)] * 4
        + [pltpu.VMEM((D_MODEL, tb), F32), pltpu.VMEM((PEER_CHUNK, tb), BF16)],
        compiler_params=_params("parallel", "arbitrary"),
    )(hnt, st, h, u_bf, vt_bf)


def _bf16_round(x):
    return x.astype(BF16).astype(F32)


def _attend_rows(q, kb, vb):
    s = jnp.sum(_bf16_round(kb) * _bf16_round(q)[None], axis=2, keepdims=True)
    p = jnp.exp(s - jnp.max(s, axis=0, keepdims=True))
    p = p / jnp.sum(p, axis=0, keepdims=True)
    return jnp.sum(_bf16_round(p) * _bf16_round(vb), axis=0)


def _mem_sample_kernel(qm_ref, mk_ref, mv_ref, o_ref):
    o_ref[0] = _attend_rows(qm_ref[0] * (MEM_HD ** -0.5), mk_ref[0], mv_ref[0])


def mem_attend_sample(qm, cache_mem_k, cache_mem_v):
    b = qm.shape[0]
    blk = pl.BlockSpec((1, MEM_HEADS, MEM_HD), lambda i: (i, 0, 0))
    cache = pl.BlockSpec((1, MEM_TOKENS, MEM_HEADS, MEM_HD), lambda i: (i, 0, 0, 0))
    return pl.pallas_call(
        _mem_sample_kernel, grid=(b,), in_specs=[blk, cache, cache], out_specs=blk,
        out_shape=jax.ShapeDtypeStruct((b, MEM_HEADS, MEM_HD), F32),
        compiler_params=_params("parallel"),
    )(qm.reshape(b, MEM_HEADS, MEM_HD), cache_mem_k, cache_mem_v)


PAGE_GROUP = 8
RING_SLOTS = 8
SCORE_ROWS = 256


def _select_tile(keys, topk):
    rows = keys.shape[0]

    def count_ge(cand):
        return jnp.sum(jnp.sum(jnp.where(keys >= cand, 1.0, 0.0), axis=0, keepdims=True), axis=1, keepdims=True)

    n_valid = count_ge(jnp.full((1, 1), INT_MIN + 1, I32))
    thr = jnp.maximum(_kth_largest_key(count_ge, n_valid, float(topk)), INT_MIN + 1)
    need = float(topk) - count_ge(thr + 1)
    tie = jnp.where(keys == thr, 1.0, 0.0)
    ri = lax.broadcasted_iota(I32, (LANES, LANES), 0)
    ci = lax.broadcasted_iota(I32, (LANES, LANES), 1)
    within = _dot(tie.astype(BF16), jnp.where(ri <= ci, 1.0, 0.0).astype(BF16))
    row_tot = jnp.broadcast_to(jnp.sum(tie, axis=1, keepdims=True), (rows, LANES))
    rr = lax.broadcasted_iota(I32, (rows, rows), 0)
    rc = lax.broadcasted_iota(I32, (rows, rows), 1)
    before = _dot(jnp.where(rc < rr, 1.0, 0.0).astype(BF16), row_tot.astype(BF16))
    take_tie = jnp.where(within + before <= need, tie, 0.0)
    return jnp.where(keys > thr, 1.0, take_tie)


def _sample_attn_kernel(pt_ref, q_ref, qi_ref, wi_ref, kin_ref, kn_ref, vn_ref, cki_ref, ck_ref, cv_ref,
                        o_ref, kibuf, ring, qb_ref, key_ref, bias_ref, satt_ref, acc_ref,
                        sem_ki, sem_ring, *, topk):
    b, nb = pl.program_id(0), pl.num_programs(0)
    n_pages = pt_ref.shape[1]
    groups = n_pages // PAGE_GROUP
    n_stream = nb * 2 * groups
    par = b % 2

    def ki_copy(bb, buf, j):
        return pltpu.make_async_copy(cki_ref.at[pt_ref[bb, j]], kibuf.at[buf, j], sem_ki.at[buf])

    def start_ki(bb, buf):
        lax.fori_loop(0, n_pages, lambda j, c: (ki_copy(bb, buf, j).start(), c)[1], 0)

    def group_copy(cache_ref, bb, g, slot, i):
        return pltpu.make_async_copy(cache_ref.at[pt_ref[bb, g * PAGE_GROUP + i]], ring.at[slot, i],
                                     sem_ring.at[slot])

    def start_group(n):
        bb, g, slot = n // (2 * groups), n % (2 * groups), n % RING_SLOTS

        @pl.when(g < groups)
        def _():
            for i in range(PAGE_GROUP):
                group_copy(ck_ref, bb, g, slot, i).start()

        @pl.when(g >= groups)
        def _():
            for i in range(PAGE_GROUP):
                group_copy(cv_ref, bb, g - groups, slot, i).start()

    def next_group(n):
        slot = n % RING_SLOTS
        for i in range(PAGE_GROUP):
            group_copy(ck_ref, 0, 0, slot, i).wait()

        @pl.when(n + RING_SLOTS - 1 < n_stream)
        def _():
            start_group(n + RING_SLOTS - 1)

        return slot

    @pl.when(b == 0)
    def _():
        start_ki(0, 0)
        for n in range(RING_SLOTS - 1):
            start_group(n)

    @pl.when(b + 1 < nb)
    def _():
        start_ki(b + 1, 1 - par)

    lax.fori_loop(0, n_pages, lambda j, c: (ki_copy(b, par, j).wait(), c)[1], 0)

    lanes = (IDX_DIM, LANES)
    for h in range(ATT_HEADS):
        qb_ref[h] = jnp.broadcast_to(q_ref[0, h], lanes)
    qi_bf = qi_ref[0].astype(BF16)
    w = wi_ref[0]
    lane = lax.broadcasted_iota(I32, (1, LANES), 1)

    def head_dots(page_of_head, qsrc):
        return jnp.concatenate([jnp.sum(page_of_head(h) * qsrc[h], axis=0, keepdims=True)
                                for h in range(ATT_HEADS)], axis=0)

    def index_row(page):
        s = _dot(qi_bf, page.astype(BF16))
        return _sortable(jnp.sum(w * jnp.maximum(s, 0.0), axis=0, keepdims=True))

    key_ref[...] = jnp.full(key_ref.shape, INT_MIN, I32)

    def score_pages(g, c):
        j0 = pl.multiple_of(g * SUBLANES, SUBLANES)
        rows = [index_row(kibuf[par, j0 + i]) for i in range(SUBLANES)]
        key_ref[pl.ds(j0, SUBLANES), :] = jnp.concatenate(rows, axis=0)
        return c

    lax.fori_loop(0, n_pages // SUBLANES, score_pages, 0)
    new_row = index_row(jnp.broadcast_to(kin_ref[0], lanes))
    key_ref[pl.ds(n_pages, 1), :] = jnp.where(lane == 0, new_row, INT_MIN)
    bias_ref[...] = jnp.where(_select_tile(key_ref[...], topk) > 0.0, 0.0, MASKED_SCORE)

    def att_row(page, j):
        satt_ref[j] = head_dots(lambda h: page[h], qb_ref) + bias_ref[pl.ds(j, 1), :]

    def k_group(g, c):
        slot = next_group(b * 2 * groups + g)
        for i in range(PAGE_GROUP):
            att_row(ring[slot, i], g * PAGE_GROUP + i)
        return c

    lax.fori_loop(0, groups, k_group, 0)
    k_new = [jnp.broadcast_to(kn_ref[0, h], lanes) for h in range(ATT_HEADS)]
    att_row(k_new, n_pages)

    m = lax.fori_loop(0, n_pages + 1, lambda j, m: jnp.maximum(m, satt_ref[j]),
                      jnp.full((ATT_HEADS, LANES), MASKED_SCORE, F32))
    m = jnp.max(m, axis=1, keepdims=True)

    acc_ref[...] = jnp.zeros(acc_ref.shape, F32)

    def add_page(page, j, psum):
        p = jnp.exp(satt_ref[j] - m)
        for h in range(ATT_HEADS):
            acc_ref[h] += page[h] * p[h:h + 1, :]
        return psum + p

    def v_group(g, psum):
        slot = next_group(b * 2 * groups + groups + g)
        for i in range(PAGE_GROUP):
            psum = add_page(ring[slot, i], g * PAGE_GROUP + i, psum)
        return psum

    psum = lax.fori_loop(0, groups, v_group, jnp.zeros((ATT_HEADS, LANES), F32))
    v_new = [jnp.broadcast_to(vn_ref[0, h], lanes) for h in range(ATT_HEADS)]
    psum = add_page(v_new, n_pages, psum)
    denom = jnp.sum(psum, axis=1, keepdims=True)
    for h in range(ATT_HEADS):
        o_ref[0, h] = jnp.sum(acc_ref[h], axis=1, keepdims=True) / denom[h:h + 1, :]


def sample_attention(q, qi, wi, ki_new, k_new, v_new, cache_idx_k, cache_k, cache_v, page_table, topk):
    b = q.shape[0]
    n_pages = page_table.shape[1]
    col = lambda a, h: a.reshape(b, h, -1, 1)
    cki = jnp.transpose(cache_idx_k, (0, 2, 1))
    ck = jnp.transpose(cache_k, (0, 2, 3, 1))
    cv = jnp.transpose(cache_v, (0, 2, 3, 1))
    head_col = pl.BlockSpec((1, ATT_HEADS, HEAD_DIM, 1), lambda i, pt: (i, 0, 0, 0))
    any_spec = pl.BlockSpec(memory_space=pl.ANY)
    grid_spec = pltpu.PrefetchScalarGridSpec(
        num_scalar_prefetch=1, grid=(b,),
        in_specs=[head_col, pl.BlockSpec((1, IDX_HEADS, IDX_DIM), lambda i, pt: (i, 0, 0)),
                  pl.BlockSpec((1, IDX_HEADS, 1), lambda i, pt: (i, 0, 0)),
                  pl.BlockSpec((1, IDX_DIM, 1), lambda i, pt: (i, 0, 0)),
                  head_col, head_col, any_spec, any_spec, any_spec],
        out_specs=head_col,
        scratch_shapes=[pltpu.VMEM((2, n_pages, IDX_DIM, PAGE_SIZE), F32),
                        pltpu.VMEM((RING_SLOTS, PAGE_GROUP, ATT_HEADS, HEAD_DIM, PAGE_SIZE), F32),
                        pltpu.VMEM((ATT_HEADS, HEAD_DIM, LANES), F32),
                        pltpu.VMEM((SCORE_ROWS, LANES), I32),
                        pltpu.VMEM((SCORE_ROWS, LANES), F32),
                        pltpu.VMEM((n_pages + 1, ATT_HEADS, LANES), F32),
                        pltpu.VMEM((ATT_HEADS, HEAD_DIM, LANES), F32),
                        pltpu.SemaphoreType.DMA((2,)), pltpu.SemaphoreType.DMA((RING_SLOTS,))])
    out = pl.pallas_call(
        functools.partial(_sample_attn_kernel, topk=topk), grid_spec=grid_spec,
        out_shape=jax.ShapeDtypeStruct((b, ATT_HEADS, HEAD_DIM, 1), F32),
        compiler_params=_params("arbitrary"),
    )(page_table, col(q, ATT_HEADS), qi.reshape(b, IDX_HEADS, IDX_DIM), wi[:, :IDX_HEADS].reshape(b, IDX_HEADS, 1),
      ki_new.reshape(b, IDX_DIM, 1), col(k_new, ATT_HEADS), col(v_new, ATT_HEADS), cki, ck, cv)
    return out.reshape(b, ATT_W)


def _pad_rows(a, rows):
    return jnp.pad(a, ((0, rows - a.shape[0]), (0, 0)))


def kernel(x_prompt, x_sample, cache_k, cache_v, cache_idx_k, cache_mem_k, cache_mem_v, state_ssm_re, state_ssm_im, page_table, mem_prompt, norm_mix, w_in, q_norm, k_norm, idx_k_norm, mq_norm, ssm_a_re, ssm_a_im, ssm_b_re, ssm_b_im, ssm_c_re, ssm_c_im, ssm_d, ssm_log_dt, w_glu, mem_norm, w_mem_kv, mk_norm, w_br_attn, w_br_ssm, w_br_mem, w_out, norm_ffn, peer_w_q, peer_sub_keys, peer_u, peer_v):
    seq = x_prompt.shape[1]
    bd = x_sample.shape[0]
    w_parts = _split_w_in(w_in)
    tables = ssm_tables(ssm_a_re, ssm_a_im, ssm_b_re, ssm_b_im, ssm_c_re, ssm_c_im, ssm_log_dt)
    u_bf = peer_u.astype(BF16)
    vt_bf = peer_v.T.astype(BF16)
    tail = (w_br_attn, w_br_ssm, w_br_mem, w_out, norm_ffn, peer_w_q, peer_sub_keys)
    norms = (q_norm, k_norm, idx_k_norm, mq_norm)

    mk, mv = mem_kv(mem_prompt[0], mem_norm, w_mem_kv, mk_norm)
    (k_p, v_p, ki_p, wi, u, gate, q_ext, qi_ext, k_bf, v_bf, ki_dup, mem_o) = project(
        x_prompt[0], jnp.arange(seq), norm_mix, w_parts, *norms, mem=(mk, mv))
    att_o = attn_prompt(qi_ext, q_ext, wi, ki_dup, k_bf, v_bf)
    ssm_o, hre_p, him_p = ssm_prompt(u, tables, ssm_d, w_glu)
    h, hn, st = merge(x_prompt[0], gate, att_o, ssm_o, mem_o, *tail)
    y_prompt = peer(hn, st, h, u_bf, vt_bf)

    (k_s, v_s, ki_s, wi, u, gate, q, qi, qm) = project(
        x_sample[:, 0], jnp.full((bd,), PAST_LEN), norm_mix, w_parts, *norms)
    att_o = sample_attention(q, qi, wi, ki_s, k_s, v_s, cache_idx_k, cache_k, cache_v, page_table,
                             min(IDX_TOPK_MAX, (PAST_LEN + 1) // 4))
    ssm_o, hre_s, him_s = ssm_sample(u, state_ssm_re, state_ssm_im, tables, ssm_d, w_glu)
    mem_o = mem_attend_sample(qm, cache_mem_k, cache_mem_v).reshape(bd, MEM_W)
    pad = lambda a: _pad_rows(a, PEER_TOKENS)
    h, hn, st = merge(pad(x_sample[:, 0]), pad(gate), pad(att_o), pad(ssm_o), pad(mem_o), *tail)
    y_sample = peer(hn, st, h, u_bf, vt_bf)[:bd]

    heads = lambda a: a.reshape(a.shape[0], ATT_HEADS, HEAD_DIM)
    state = lambda a, n: a.reshape(n, SSM_GROUPS, SSM_STATE)
    return (y_prompt[None], y_sample[:, None],
            heads(k_p)[None], heads(v_p)[None], ki_p[None],
            mk.reshape(1, MEM_TOKENS, MEM_HEADS, MEM_HD), mv.reshape(1, MEM_TOKENS, MEM_HEADS, MEM_HD),
            state(hre_p, 1), state(him_p, 1),
            heads(k_s)[:, None], heads(v_s)[:, None], ki_s[:, None],
            state(hre_s, bd), state(him_s, bd))
```

```python
import functools
import math

import jax
import jax.numpy as jnp
import numpy as np
from jax import lax
from jax.experimental import pallas as pl
from jax.experimental.pallas import tpu as pltpu

F32 = jnp.float32
BF16 = jnp.bfloat16
I32 = jnp.int32

D_MODEL = 1024
PAST_LEN = 16384
PAGE_SIZE = 128
ATT_HEADS = 8
HEAD_DIM = 64
ATT_W = ATT_HEADS * HEAD_DIM
IDX_HEADS = 8
IDX_DIM = 64
IDX_TOPK_MAX = 256
ROPE_THETA = 500000.0
ROPE_HALF = HEAD_DIM // 4 // 2
SSM_GROUP = 16
SSM_W = 512
SSM_GROUPS = SSM_W // SSM_GROUP
SSM_STATE = 64
SSM_LANES = SSM_GROUPS * SSM_STATE
MEM_TOKENS = 256
MEM_HEADS = 4
MEM_HD = 128
MEM_W = MEM_HEADS * MEM_HD
N_BRANCH = 3
PEER_HEADS = 8
N_KEYS = 128
N_EXPERTS = N_KEYS * N_KEYS
PEER_DK = 128
PEER_TOPK = 16
EPS = 1e-6

LANES = 128
SUBLANES = 8
VMEM_LIMIT = 56 * 1024 * 1024
INT_MIN = -(2 ** 31)
MASKED_SCORE = -1e30
RUNNING_MAX_INIT = -5e29

NT_DIMS = (((1,), (1,)), ((), ()))
LOG2_E = math.log2(math.e)


def _params(*sem):
    return pltpu.CompilerParams(dimension_semantics=sem, vmem_limit_bytes=VMEM_LIMIT)


def _full(shape):
    nd = len(shape)
    return pl.BlockSpec(shape, lambda *_: (0,) * nd, pipeline_mode=pl.Buffered(1))


def _rms(x, g, n):
    ms = jnp.sum(x * x, axis=-1, keepdims=True) * (1.0 / n)
    return x * lax.rsqrt(ms + EPS) * g


def _dot(a, b):
    return jnp.dot(a, b, preferred_element_type=F32)


def _dot_nt(a, b):
    return lax.dot_general(a, b, NT_DIMS, preferred_element_type=F32)


def _mem_kv_kernel(mem_ref, mn_ref, w_ref, mkn_ref, mk_ref, mv_ref):
    xn = _rms(mem_ref[...], mn_ref[...], D_MODEL).astype(BF16)
    y = _dot(xn, w_ref[...])
    for h in range(MEM_HEADS):
        sl = slice(h * MEM_HD, (h + 1) * MEM_HD)
        mk_ref[:, sl] = _rms(y[:, sl], mkn_ref[...], MEM_HD)
    mv_ref[...] = y[:, MEM_W:]


def mem_kv(mem, mem_norm, w_mem_kv, mk_norm):
    m = mem.shape[0]
    return pl.pallas_call(
        _mem_kv_kernel,
        out_shape=(jax.ShapeDtypeStruct((m, MEM_W), F32), jax.ShapeDtypeStruct((m, MEM_W), F32)),
        compiler_params=pltpu.CompilerParams(vmem_limit_bytes=VMEM_LIMIT),
    )(mem, mem_norm.reshape(1, -1), w_mem_kv.astype(BF16), mk_norm.reshape(1, -1))


def _rope_tables(pos):
    inv_freq = ROPE_THETA ** (-jnp.arange(ROPE_HALF, dtype=F32) / ROPE_HALF)
    ang = pos.astype(F32)[:, None] * inv_freq[None, :]
    cos, sin = jnp.cos(ang), jnp.sin(ang)
    t = pos.shape[0]
    rest = HEAD_DIM - 2 * ROPE_HALF
    c = jnp.concatenate([cos, cos, jnp.ones((t, rest), F32)], axis=1)
    s_hi = jnp.concatenate([-sin, jnp.zeros((t, HEAD_DIM - ROPE_HALF), F32)], axis=1)
    s_lo = jnp.concatenate([jnp.zeros((t, ROPE_HALF), F32), sin, jnp.zeros((t, rest), F32)], axis=1)
    rep = LANES // HEAD_DIM
    return jnp.tile(c, (1, rep)), jnp.tile(s_hi, (1, rep)), jnp.tile(s_lo, (1, rep))


def _rope(y, c, s_hi, s_lo):
    w = y.shape[1]
    rep = w // LANES
    if rep > 1:
        c = jnp.concatenate([c] * rep, axis=1)
        s_hi = jnp.concatenate([s_hi] * rep, axis=1)
        s_lo = jnp.concatenate([s_lo] * rep, axis=1)
    return y * c + pltpu.roll(y, w - ROPE_HALF, axis=1) * s_hi + pltpu.roll(y, ROPE_HALF, axis=1) * s_lo


def _group_rms(y, grp, g, n):
    sq = y * y
    hi = sq.astype(BF16)
    lo = (sq - hi.astype(F32)).astype(BF16)
    ss = _dot(hi, grp) + _dot(lo, grp)
    return y * lax.rsqrt(ss * (1.0 / n) + EPS) * g


def _proj_kernel(*refs, prompt):
    (x_ref, nm_ref, wq_ref, wk_ref, wv_ref, wqi_ref, wki_ref, wwi_ref, wu_ref, wqm_ref, wg_ref,
     qn_ref, kn_ref, ikn_ref, mqn_ref, grp_ref, c_ref, shi_ref, slo_ref) = refs[:19]
    refs = refs[19:]
    if prompt:
        mk_ref, mv_ref = refs[:2]
        (k_out, v_out, ki_out, wi_out, u_out, gate_out,
         qx_out, qix_out, kbf_out, vbf_out, kid_out, memo_out) = refs[2:]
    else:
        k_out, v_out, ki_out, wi_out, u_out, gate_out, q_out, qi_out, qm_out = refs

    xn = _rms(x_ref[...], nm_ref[...], D_MODEL).astype(BF16)
    c, s_hi, s_lo = c_ref[...], shi_ref[...], slo_ref[...]
    grp = grp_ref[...]

    q = _rope(_group_rms(_dot(xn, wq_ref[...]), grp, qn_ref[...], HEAD_DIM), c, s_hi, s_lo)
    q = q * (HEAD_DIM ** -0.5)
    k = _rope(_group_rms(_dot(xn, wk_ref[...]), grp, kn_ref[...], HEAD_DIM), c, s_hi, s_lo)
    v = _dot(xn, wv_ref[...])
    qi = _rope(_dot(xn, wqi_ref[...]), c, s_hi, s_lo)
    ki2 = _dot(xn, wki_ref[...])
    ki2 = _rope(_rms(ki2, ikn_ref[...], 2 * IDX_DIM), c, s_hi, s_lo)
    wi = _dot(xn, wwi_ref[...]) * (IDX_HEADS ** -0.5 * IDX_DIM ** -0.5)
    qm = _dot(xn, wqm_ref[...])
    qm = jnp.concatenate(
        [_rms(qm[:, h * MEM_HD:(h + 1) * MEM_HD], mqn_ref[...], MEM_HD) for h in range(MEM_HEADS)], axis=1)

    k_out[...] = k
    v_out[...] = v
    ki_out[...] = ki2[:, :IDX_DIM]
    wi_out[...] = wi
    u_out[...] = _dot(xn, wu_ref[...])
    gate_out[...] = jax.nn.sigmoid(_dot(xn, wg_ref[...]))

    if not prompt:
        q_out[...] = q
        qi_out[...] = qi
        qm_out[...] = qm
        return

    kbf_out[...] = k.astype(BF16)
    vbf_out[...] = v.astype(BF16)
    kid_out[...] = ki2.astype(BF16)
    lane = lax.broadcasted_iota(I32, (q.shape[0], LANES), 1)
    for h in range(ATT_HEADS):
        pair = slice((h // 2) * LANES, (h // 2 + 1) * LANES)
        own = (lane < HEAD_DIM) if h % 2 == 0 else (lane >= HEAD_DIM)
        qx_out[h] = jnp.where(own, q[:, pair] * LOG2_E, 0.0).astype(BF16)
        qix_out[h] = jnp.where(own, qi[:, pair], 0.0).astype(BF16)

    for h in range(MEM_HEADS):
        sl = slice(h * MEM_HD, (h + 1) * MEM_HD)
        s = _dot_nt(qm[:, sl].astype(BF16), mk_ref[:, sl]) * (MEM_HD ** -0.5)
        p = jnp.exp(s - jnp.max(s, axis=-1, keepdims=True))
        p = p / jnp.sum(p, axis=-1, keepdims=True)
        memo_out[:, sl] = _dot(p.astype(BF16), mv_ref[:, sl])


def _split_w_in(w_in):
    splits = np.cumsum([ATT_W, ATT_W, ATT_W, IDX_HEADS * IDX_DIM, IDX_DIM, IDX_HEADS, SSM_W, MEM_W])
    wq, wk, wv, wqi, wki, wwi, wu, wqm, wg = jnp.split(w_in.astype(BF16), splits.tolist(), axis=1)
    wki = jnp.concatenate([wki, wki], axis=1)
    wwi = jnp.pad(wwi, ((0, 0), (0, LANES - IDX_HEADS)))
    return wq, wk, wv, wqi, wki, wwi, wu, wqm, wg


def project(x, pos, norm_mix, w_parts, q_norm, k_norm, idx_k_norm, mq_norm, mem=None, tile=256):
    t = x.shape[0]
    tile = min(tile, t)
    prompt = mem is not None
    heads_per_slab = ATT_W // HEAD_DIM
    grp = jnp.kron(jnp.eye(heads_per_slab, dtype=F32), jnp.ones((HEAD_DIM, HEAD_DIM), F32)).astype(BF16)
    c, s_hi, s_lo = _rope_tables(pos)
    tile_rep = lambda g, n: jnp.tile(g.reshape(1, -1), (1, n))
    ins = [x, norm_mix.reshape(1, -1), *w_parts,
           tile_rep(q_norm, ATT_HEADS), tile_rep(k_norm, ATT_HEADS), tile_rep(idx_k_norm, 2),
           mq_norm.reshape(1, -1), grp, c, s_hi, s_lo]
    row = lambda w: pl.BlockSpec((tile, w), lambda i: (i, 0))
    in_specs = [row(D_MODEL)] + [_full(a.shape) for a in ins[1:16]] + [row(LANES)] * 3
    outs = [(ATT_W, F32), (ATT_W, F32), (IDX_DIM, F32), (LANES, F32), (SSM_W, F32), (N_BRANCH * D_MODEL, F32)]
    out_shape = [jax.ShapeDtypeStruct((t, w), d) for w, d in outs]
    out_specs = [row(w) for w, _ in outs]
    if prompt:
        mk, mv = mem
        ins += [mk.astype(BF16), mv.astype(BF16)]
        in_specs += [_full(mk.shape), _full(mv.shape)]
        ext = pl.BlockSpec((ATT_HEADS, tile, LANES), lambda i: (0, i, 0))
        out_shape += [jax.ShapeDtypeStruct((ATT_HEADS, t, LANES), BF16)] * 2
        out_specs += [ext, ext]
        for w, d in [(ATT_W, BF16), (ATT_W, BF16), (LANES, BF16), (MEM_W, F32)]:
            out_shape.append(jax.ShapeDtypeStruct((t, w), d))
            out_specs.append(row(w))
    else:
        for w in (ATT_W, IDX_HEADS * IDX_DIM, MEM_W):
            out_shape.append(jax.ShapeDtypeStruct((t, w), F32))
            out_specs.append(row(w))
    return pl.pallas_call(
        functools.partial(_proj_kernel, prompt=prompt),
        grid=(t // tile,),
        in_specs=in_specs, out_specs=out_specs, out_shape=out_shape,
        compiler_params=_params("parallel"),
    )(*ins)


def _sortable(x):
    b = pltpu.bitcast(jnp.where(x == 0.0, 0.0, x), I32)
    return jnp.where(b < 0, b ^ 0x7FFFFFFF, b)


def _kth_largest_key(count_ge, n_valid, k):
    c0 = count_ge(jnp.zeros(n_valid.shape, I32))
    pos = c0 >= k
    ans, cnt = jnp.where(pos, 0, INT_MIN).astype(I32), jnp.where(pos, c0, n_valid)

    def pending(cnt):
        return jnp.max(jnp.where((cnt != k) & (n_valid >= k), 1.0, 0.0))

    def body(state):
        t, _, ans, cnt = state
        cand = ans | jnp.left_shift(jnp.int32(1), 30 - t)
        c = count_ge(cand)
        up = c >= k
        ans, cnt = jnp.where(up, cand, ans), jnp.where(up, c, cnt)
        return t + 1, pending(cnt), ans, cnt

    state = (jnp.int32(0), pending(cnt), ans, cnt)
    return lax.while_loop(lambda s: (s[0] < 31) & (s[1] > 0.0), body, state)[2]


def _attn_prompt_kernel(qi_ref, q_ref, wi_ref, kid_ref, k_ref, v_ref, o_ref,
                        s_ref, m_ref, l_ref, acc_ref, bias_ref, qk_ref, p_even, p_odd, *, tq, tk, topk):
    q0 = pl.program_id(0) * tq
    nkc = (q0 + tq + tk - 1) // tk
    row = q0 + lax.broadcasted_iota(I32, (tq, tk), 0)
    col0 = lax.broadcasted_iota(I32, (tq, tk), 1)
    wi = wi_ref[...]

    def score_chunk(kc, carry):
        off = pl.multiple_of(kc * tk, tk)
        kd = kid_ref[pl.ds(off, tk), :]
        acc = jnp.zeros((tq, tk), F32)
        for h in range(IDX_HEADS):
            acc = acc + wi[:, h:h + 1] * jnp.maximum(_dot_nt(qi_ref[h], kd), 0.0)
        s_ref[kc] = jnp.where(col0 + off <= row, _sortable(acc), INT_MIN)
        return carry

    lax.fori_loop(0, nkc, score_chunk, 0)

    def count_ge(cand):
        candb = jnp.broadcast_to(cand, (tq, LANES))

        def body(kc, cnt):
            blk = s_ref[kc]
            for j in range(tk // LANES):
                cnt = cnt + jnp.where(blk[:, j * LANES:(j + 1) * LANES] >= candb, 1.0, 0.0)
            return cnt

        cnt = lax.fori_loop(0, nkc, body, jnp.zeros((tq, LANES), F32))
        return jnp.sum(cnt, axis=1, keepdims=True)

    n_valid = (q0 + 1 + lax.broadcasted_iota(I32, (tq, 1), 0)).astype(F32)
    thr = jnp.maximum(_kth_largest_key(count_ge, n_valid, float(topk)), INT_MIN + 1)

    m_ref[...] = jnp.full(m_ref.shape, RUNNING_MAX_INIT, F32)
    l_ref[...] = jnp.zeros(l_ref.shape, F32)
    acc_ref[...] = jnp.zeros(acc_ref.shape, F32)

    cols = [slice(j * LANES, (j + 1) * LANES) for j in range(tk // LANES)]

    pairs = [slice((h // 2) * LANES, (h // 2 + 1) * LANES) for h in range(ATT_HEADS)]

    def apply_values(p_ref, kc):
        off = pl.multiple_of(kc * tk, tk)
        for h in range(ATT_HEADS):
            acc_ref[h] += _dot(p_ref[h], v_ref[pl.ds(off, tk), pairs[h]])

    def chunk_step(kc, p_new, p_prev):
        off = pl.multiple_of(kc * tk, tk)
        bias_ref[...] = jnp.where(s_ref[kc] >= thr, 0.0, MASKED_SCORE)
        off_prev = pl.multiple_of(jnp.maximum(kc - 1, 0) * tk, tk)
        for h in range(ATT_HEADS):
            qk_ref[h] = _dot_nt(q_ref[h], k_ref[pl.ds(off, tk), pairs[h]]) + bias_ref[...]
            acc_ref[h] += _dot(p_prev[h], v_ref[pl.ds(off_prev, tk), pairs[h]])
        for h in range(ATT_HEADS):
            mx = functools.reduce(jnp.maximum, [qk_ref[h, :, c] for c in cols])
            m_old = m_ref[h]
            m_new = jnp.maximum(m_old, jnp.max(mx, axis=1, keepdims=True))
            alpha = jnp.exp2(m_old - m_new)
            psum = jnp.zeros((tq, LANES), F32)
            for c in cols:
                p = jnp.exp2(qk_ref[h, :, c] - m_new)
                psum = psum + p
                p_new[h, :, c] = p.astype(BF16)
            m_ref[h] = m_new
            l_ref[h] = alpha * l_ref[h] + psum
            acc_ref[h] = alpha * acc_ref[h]

    p_odd[...] = jnp.zeros(p_odd.shape, BF16)

    def attend_chunk(kc, carry):
        @pl.when(kc % 2 == 0)
        def _():
            chunk_step(kc, p_even, p_odd)

        @pl.when(kc % 2 == 1)
        def _():
            chunk_step(kc, p_odd, p_even)

        return carry

    lax.fori_loop(0, nkc, attend_chunk, 0)

    @pl.when(nkc % 2 == 1)
    def _():
        apply_values(p_even, nkc - 1)

    @pl.when(nkc % 2 == 0)
    def _():
        apply_values(p_odd, nkc - 1)

    lane = lax.broadcasted_iota(I32, (tq, LANES), 1)
    for j in range(ATT_HEADS // 2):
        even = acc_ref[2 * j] / jnp.sum(l_ref[2 * j], axis=1, keepdims=True)
        odd = acc_ref[2 * j + 1] / jnp.sum(l_ref[2 * j + 1], axis=1, keepdims=True)
        o_ref[:, j * LANES:(j + 1) * LANES] = jnp.where(lane < HEAD_DIM, even, odd)


def attn_prompt(qi_ext, q_ext, wi, ki_dup, k_bf, v_bf, tq=128, tk=512):
    t = k_bf.shape[0]
    tq, tk = min(tq, t), min(tk, t)
    topk = min(IDX_TOPK_MAX, t // 4)
    ext = pl.BlockSpec((ATT_HEADS, tq, LANES), lambda i: (0, i, 0))
    return pl.pallas_call(
        functools.partial(_attn_prompt_kernel, tq=tq, tk=tk, topk=topk),
        grid=(t // tq,),
        in_specs=[ext, ext, pl.BlockSpec((tq, LANES), lambda i: (i, 0)),
                  _full(ki_dup.shape), _full(k_bf.shape), _full(v_bf.shape)],
        out_specs=pl.BlockSpec((tq, ATT_W), lambda i: (i, 0)),
        out_shape=jax.ShapeDtypeStruct((t, ATT_W), F32),
        scratch_shapes=[pltpu.VMEM((t // tk, tq, tk), I32),
                        pltpu.VMEM((ATT_HEADS, tq, LANES), F32),
                        pltpu.VMEM((ATT_HEADS, tq, LANES), F32),
                        pltpu.VMEM((ATT_HEADS, tq, LANES), F32),
                        pltpu.VMEM((tq, tk), F32),
                        pltpu.VMEM((ATT_HEADS, tq, tk), F32),
                        pltpu.VMEM((ATT_HEADS, tq, tk), BF16),
                        pltpu.VMEM((ATT_HEADS, tq, tk), BF16)],
        compiler_params=_params("arbitrary"),
    )(qi_ext, q_ext, wi, ki_dup, k_bf, v_bf)


SSM_SEG = 32
SSM_CHUNK = SUBLANES * SSM_SEG


def _cexp(re, im):
    mag = jnp.exp(re)
    return mag * jnp.cos(im), mag * jnp.sin(im)


def _ssm_prep_kernel(are_ref, aim_ref, ldt_ref, are16_ref, aim16_ref, bre_ref, bim_ref,
                     bbre_ref, bbim_ref, pre_ref, pim_ref):
    dt = jnp.exp(ldt_ref[...])
    ar, ai = are16_ref[...], aim16_ref[...]
    er, ei = _cexp(ar * dt, ai * dt)
    nr, ni = er - 1.0, ei
    den = ar * ar + ai * ai
    cr, ci = (nr * ar + ni * ai) / den, (ni * ar - nr * ai) / den
    br, bi = bre_ref[...], bim_ref[...]
    bbre_ref[...] = cr * br - ci * bi
    bbim_ref[...] = cr * bi + ci * br
    ar, ai = are_ref[...] * dt, aim_ref[...] * dt
    for j in range(SSM_SEG):
        pr, pi = _cexp(ar * (j + 1.0), ai * (j + 1.0))
        pre_ref[j] = pr
        pim_ref[j] = pi


def ssm_tables(a_re, a_im, b_re, b_im, c_re, c_im, log_dt):
    g, p, c = SSM_GROUPS, SSM_STATE, SSM_GROUP
    rep = lambda a: jnp.repeat(a, c, axis=1)
    out_shape = ([jax.ShapeDtypeStruct((g, p * c), F32)] * 2
                 + [jax.ShapeDtypeStruct((SSM_SEG, g, p), F32)] * 2)
    bbre, bbim, pre, pim = pl.pallas_call(
        _ssm_prep_kernel, out_shape=out_shape,
        compiler_params=pltpu.CompilerParams(vmem_limit_bytes=VMEM_LIMIT),
    )(a_re, a_im, log_dt.reshape(g, 1), rep(a_re), rep(a_im), b_re.reshape(g, p * c), b_im.reshape(g, p * c))
    eye = jnp.eye(g, dtype=F32)
    blockdiag_in = lambda bb: jnp.einsum('gpc,gh->gchp', bb.reshape(g, p, c), eye).reshape(g * c, g * p)
    blockdiag_out = lambda cc: jnp.einsum('gcp,gh->gphc', cc, eye).reshape(g * p, g * c)
    bmat = jnp.concatenate([blockdiag_in(bbre), blockdiag_in(bbim)], axis=1)
    cmat = jnp.concatenate([blockdiag_out(c_re), blockdiag_out(-c_im)], axis=0)
    return bmat, cmat, pre.reshape(SSM_SEG, g * p), pim.reshape(SSM_SEG, g * p)


def _ssm_out(h_all, u, cmat_ref, d_ref, wglu_ref):
    y = _dot(h_all.astype(BF16), cmat_ref[...]) + d_ref[...] * u
    z = jax.nn.gelu(y)
    return z * jax.nn.sigmoid(_dot(z.astype(BF16), wglu_ref[...]))


def _ssm_prompt_kernel(u_ref, bmat_ref, cmat_ref, pre_ref, pim_ref, d_ref, wglu_ref,
                       o_ref, hre_ref, him_ref, bu_ref, h_ref, carry_ref):
    n = SSM_LANES

    @pl.when(pl.program_id(0) == 0)
    def _():
        carry_ref[...] = jnp.zeros(carry_ref.shape, F32)

    u = u_ref[...]
    bu = _dot(u.astype(BF16), bmat_ref[...])
    n_blk = n // LANES
    for b in range(2 * n_blk):
        bu_ref[b] = bu[:, b * LANES:(b + 1) * LANES]

    group = 8
    for b0 in range(0, n_blk, group):
        a = [(jnp.broadcast_to(pre_ref[0:1, pl.ds(b * LANES, LANES)], (SUBLANES, LANES)),
              jnp.broadcast_to(pim_ref[0:1, pl.ds(b * LANES, LANES)], (SUBLANES, LANES)))
             for b in range(b0, b0 + group)]

        def step(j, carry):
            rows = pl.ds(pl.multiple_of(j * SUBLANES, SUBLANES), SUBLANES)
            out = []
            for g in range(group):
                (a_re, a_im), (h_re, h_im) = a[g], carry[g]
                n_re = a_re * h_re - a_im * h_im + bu_ref[b0 + g, rows, :]
                n_im = a_re * h_im + a_im * h_re + bu_ref[n_blk + b0 + g, rows, :]
                h_ref[b0 + g, rows, :] = n_re
                h_ref[n_blk + b0 + g, rows, :] = n_im
                out.append((n_re, n_im))
            return tuple(out)

        zero = jnp.zeros((SUBLANES, LANES), F32)
        lax.fori_loop(0, SSM_SEG, step, ((zero, zero),) * group)

    last = pl.ds((SSM_SEG - 1) * SUBLANES, SUBLANES)
    for b in range(n_blk):
        re_sl, im_sl = pl.ds(b * LANES, LANES), pl.ds(n + b * LANES, LANES)
        e_re, e_im = h_ref[b, last, :], h_ref[n_blk + b, last, :]
        w_re, w_im = pre_ref[SSM_SEG - 1:SSM_SEG, re_sl], pim_ref[SSM_SEG - 1:SSM_SEG, re_sl]
        f_re, f_im = carry_ref[:, re_sl], carry_ref[:, im_sl]
        enter_re, enter_im = [], []
        for s in range(SUBLANES):
            enter_re.append(f_re)
            enter_im.append(f_im)
            f_re, f_im = (e_re[s:s + 1] + w_re * f_re - w_im * f_im,
                          e_im[s:s + 1] + w_re * f_im + w_im * f_re)
        carry_ref[:, re_sl] = f_re
        carry_ref[:, im_sl] = f_im
        g_re, g_im = jnp.concatenate(enter_re, axis=0), jnp.concatenate(enter_im, axis=0)
        for j in range(SSM_SEG):
            rows = pl.ds(j * SUBLANES, SUBLANES)
            p_re, p_im = pre_ref[j:j + 1, re_sl], pim_ref[j:j + 1, re_sl]
            h_ref[b, rows, :] += p_re * g_re - p_im * g_im
            h_ref[n_blk + b, rows, :] += p_re * g_im + p_im * g_re

    h_all = jnp.concatenate([h_ref[b] for b in range(2 * n_blk)], axis=1)
    o_ref[...] = _ssm_out(h_all, u, cmat_ref, d_ref, wglu_ref)
    hre_ref[...] = carry_ref[:, :n]
    him_ref[...] = carry_ref[:, n:]


def ssm_prompt(u, tables, d, w_glu):
    t = u.shape[0]
    bmat, cmat, pre, pim = tables
    n = SSM_LANES
    n_chunks = t // SSM_CHUNK
    to_steps = lambda a: a.reshape(n_chunks, SUBLANES, SSM_SEG, -1).swapaxes(1, 2).reshape(t, -1)
    to_time = lambda a: a.reshape(n_chunks, SSM_SEG, SUBLANES, -1).swapaxes(1, 2).reshape(t, -1)
    ins = [to_steps(u), bmat.astype(BF16), cmat.astype(BF16), pre, pim, d.reshape(1, -1), w_glu.astype(BF16)]
    o, h_re, h_im = pl.pallas_call(
        _ssm_prompt_kernel,
        grid=(t // SSM_CHUNK,),
        in_specs=[pl.BlockSpec((SSM_CHUNK, SSM_W), lambda i: (i, 0))] + [_full(a.shape) for a in ins[1:]],
        out_specs=[pl.BlockSpec((SSM_CHUNK, SSM_W), lambda i: (i, 0)),
                   pl.BlockSpec((1, n), lambda i: (0, 0)), pl.BlockSpec((1, n), lambda i: (0, 0))],
        out_shape=[jax.ShapeDtypeStruct((t, SSM_W), F32),
                   jax.ShapeDtypeStruct((1, n), F32), jax.ShapeDtypeStruct((1, n), F32)],
        scratch_shapes=[pltpu.VMEM((2 * n // LANES, SSM_CHUNK, LANES), F32)] * 2 + [pltpu.VMEM((1, 2 * n), F32)],
        compiler_params=_params("arbitrary"),
    )(*ins)
    return to_time(o), h_re, h_im


def _ssm_sample_kernel(u_ref, h0re_ref, h0im_ref, bmat_ref, cmat_ref, pre_ref, pim_ref, d_ref, wglu_ref,
                       o_ref, hre_ref, him_ref):
    n = SSM_LANES
    u = u_ref[...]
    bu = jnp.dot(u, bmat_ref[...], preferred_element_type=F32, precision=lax.Precision.HIGHEST)
    a_re, a_im = pre_ref[0:1, :], pim_ref[0:1, :]
    h0_re, h0_im = h0re_ref[...], h0im_ref[...]
    h_re = bu[:, :n] + a_re * h0_re - a_im * h0_im
    h_im = bu[:, n:] + a_re * h0_im + a_im * h0_re
    hre_ref[...] = h_re
    him_ref[...] = h_im
    o_ref[...] = _ssm_out(jnp.concatenate([h_re, h_im], axis=1), u, cmat_ref, d_ref, wglu_ref)


def ssm_sample(u, h0_re, h0_im, tables, d, w_glu):
    b = u.shape[0]
    bmat, cmat, pre, pim = tables
    n = SSM_LANES
    return pl.pallas_call(
        _ssm_sample_kernel,
        out_shape=[jax.ShapeDtypeStruct((b, SSM_W), F32),
                   jax.ShapeDtypeStruct((b, n), F32), jax.ShapeDtypeStruct((b, n), F32)],
        compiler_params=pltpu.CompilerParams(vmem_limit_bytes=VMEM_LIMIT),
    )(u, h0_re.reshape(b, n), h0_im.reshape(b, n), bmat, cmat.astype(BF16), pre, pim,
      d.reshape(1, -1), w_glu.astype(BF16))


def _merge_kernel(x_ref, gate_ref, att_ref, ssm_ref, mem_ref, wa_ref, ws_ref, wm_ref, wo_ref,
                  nf_ref, wq_ref, skt_ref, h_out, hnt_out, st_out):
    g = gate_ref[...]
    mixed = (g[:, :D_MODEL] * _dot(att_ref[...].astype(BF16), wa_ref[...])
             + g[:, D_MODEL:2 * D_MODEL] * _dot(ssm_ref[...].astype(BF16), ws_ref[...])
             + g[:, 2 * D_MODEL:] * _dot(mem_ref[...].astype(BF16), wm_ref[...]))
    h = x_ref[...] + _dot(mixed.astype(BF16), wo_ref[...])
    hn = _rms(h, nf_ref[...], D_MODEL)
    pq = _dot(hn.astype(BF16), wq_ref[...]).astype(BF16)
    h_out[...] = h
    hnt_out[...] = hn.T.astype(BF16)
    st_out[...] = _dot_nt(skt_ref[...], pq)


def _sub_key_matrix(sub_keys):
    nb = PEER_HEADS * 2
    sk = sub_keys.reshape(nb, N_KEYS, PEER_DK // 2)
    eye = jnp.eye(nb, dtype=sk.dtype)
    return jnp.einsum('bnd,bc->bncd', sk, eye).reshape(nb * N_KEYS, nb * (PEER_DK // 2))


def merge(x, gate, att_o, ssm_o, mem_o, w_br_attn, w_br_ssm, w_br_mem, w_out, norm_ffn, peer_w_q,
          sub_keys, tile=256):
    t = x.shape[0]
    ws = [w.astype(BF16) for w in (w_br_attn, w_br_ssm, w_br_mem, w_out)]
    ins = [x, gate, att_o, ssm_o, mem_o, *ws, norm_ffn.reshape(1, -1), peer_w_q.astype(BF16),
           _sub_key_matrix(sub_keys).astype(BF16)]
    row = lambda w: pl.BlockSpec((tile, w), lambda i: (i, 0))
    n_scores = PEER_HEADS * 2 * N_KEYS
    return pl.pallas_call(
        _merge_kernel,
        grid=(t // tile,),
        in_specs=[row(D_MODEL), row(N_BRANCH * D_MODEL), row(ATT_W), row(SSM_W), row(MEM_W)]
        + [_full(a.shape) for a in ins[5:]],
        out_specs=[row(D_MODEL), pl.BlockSpec((D_MODEL, tile), lambda i: (0, i)),
                   pl.BlockSpec((n_scores, tile), lambda i: (0, i))],
        out_shape=[jax.ShapeDtypeStruct((t, D_MODEL), F32), jax.ShapeDtypeStruct((D_MODEL, t), BF16),
                   jax.ShapeDtypeStruct((n_scores, t), F32)],
        compiler_params=_params("parallel"),
    )(*ins)


PEER_TOKENS = 256
PEER_CHUNK = 2048
PEER_VALUE_PARTS = 8
NEG_INF = float("-inf")


def _top_values(x, k):
    vals = []
    for _ in range(k):
        m = jnp.max(x, axis=0, keepdims=True)
        vals.append(m)
        x = jnp.where(x == m, NEG_INF, x)
    return vals


def _peer_route(st_ref, cut_ref, s2_ref, e2_ref, f1_ref):
    tokens = st_ref.shape[1]
    for h in range(PEER_HEADS):
        s1 = st_ref[pl.ds(2 * h * N_KEYS, N_KEYS), :]
        s2 = st_ref[pl.ds((2 * h + 1) * N_KEYS, N_KEYS), :]
        a = _top_values(s1, PEER_TOPK)
        b = _top_values(s2, PEER_TOPK)
        cand = [a[i] + b[j] for i in range(PEER_TOPK) for j in range(PEER_TOPK) if (i + 1) * (j + 1) <= PEER_TOPK]
        pad = -len(cand) % SUBLANES
        cand = jnp.concatenate(cand + [jnp.full((pad, tokens), NEG_INF, F32)], axis=0)
        thr = _top_values(cand, PEER_TOPK)[-1]
        top = a[0] + b[0]
        z = jnp.sum(jnp.where(cand >= thr, jnp.exp(cand - top), 0.0), axis=0, keepdims=True)
        cut = jnp.full(s1.shape, jnp.inf, F32)
        for bl in b:
            cut = jnp.where(s1 + bl >= thr, bl, cut)
        cut_ref[h] = cut
        s2_ref[h] = s2
        e2_ref[h] = jnp.exp(s2 - b[0])
        f1_ref[h] = jnp.exp(s1 - a[0]) / z


def _peer_kernel(hnt_ref, st_ref, h_ref, u_ref, vt_ref, y_ref,
                 cut_ref, s2_ref, e2_ref, f1_ref, acc_ref, pt_even, pt_odd):
    e = pl.program_id(1)
    n_chunks = pl.num_programs(1) - 1
    tokens = hnt_ref.shape[1]
    rows_per_step = PEER_CHUNK // N_KEYS

    @pl.when(e == 0)
    def _():
        _peer_route(st_ref, cut_ref, s2_ref, e2_ref, f1_ref)
        acc_ref[...] = jnp.zeros(acc_ref.shape, F32)
        pt_odd[...] = jnp.zeros(pt_odd.shape, BF16)

    value_rows = D_MODEL // PEER_VALUE_PARTS

    def apply_values(pt_ref, part):
        rows = slice(part * value_rows, (part + 1) * value_rows)
        acc_ref[rows, :] += _dot(vt_ref[rows, :], pt_ref[...])

    def build_and_apply(pt_new, pt_prev):
        tiles_per_part = rows_per_step // PEER_VALUE_PARTS
        for il in range(rows_per_step):
            i = e * rows_per_step + il
            sl = slice(il * N_KEYS, (il + 1) * N_KEYS)
            w = jnp.zeros((N_KEYS, tokens), F32)
            for h in range(PEER_HEADS):
                val = e2_ref[h] * f1_ref[h, pl.ds(i, 1), :]
                w = w + jnp.where(s2_ref[h] >= cut_ref[h, pl.ds(i, 1), :], val, 0.0)
            act = jax.nn.gelu(_dot(u_ref[sl, :], hnt_ref[...]))
            pt_new[sl, :] = (w * act).astype(BF16)
            if il % tiles_per_part == tiles_per_part - 1:
                apply_values(pt_prev, il // tiles_per_part)

    @pl.when((e < n_chunks) & (e % 2 == 0))
    def _():
        build_and_apply(pt_even, pt_odd)

    @pl.when((e < n_chunks) & (e % 2 == 1))
    def _():
        build_and_apply(pt_odd, pt_even)

    @pl.when(e == n_chunks)
    def _():
        for part in range(PEER_VALUE_PARTS):
            apply_values(pt_even if (N_EXPERTS // PEER_CHUNK) % 2 == 1 else pt_odd, part)
        y_ref[...] = h_ref[...] + acc_ref[...].T


def peer(hnt, st, h, u_bf, vt_bf):
    t = hnt.shape[1]
    tb = PEER_TOKENS
    n_scores = st.shape[0]
    n_chunks = N_EXPERTS // PEER_CHUNK
    return pl.pallas_call(
        _peer_kernel,
        grid=(t // tb, n_chunks + 1),
        in_specs=[pl.BlockSpec((D_MODEL, tb), lambda i, e: (0, i)),
                  pl.BlockSpec((n_scores, tb), lambda i, e: (0, i)),
                  pl.BlockSpec((tb, D_MODEL), lambda i, e: (i, 0)),
                  pl.BlockSpec((PEER_CHUNK, D_MODEL), lambda i, e: (jnp.minimum(e, n_chunks - 1), 0)),
                  pl.BlockSpec((D_MODEL, PEER_CHUNK), lambda i, e: (0, jnp.maximum(e - 1, 0)))],
        out_specs=pl.BlockSpec((tb, D_MODEL), lambda i, e: (i, 0)),
        out_shape=jax.ShapeDtypeStruct((t, D_MODEL), F32),
        scratch_shapes=[pltpu.VMEM((PEER_HEADS, N_KEYS, tb), F32)] * 4
        + [pltpu.VMEM((D_MODEL, tb), F32)] + [pltpu.VMEM((PEER_CHUNK, tb), BF16)] * 2,
        compiler_params=_params("parallel", "arbitrary"),
    )(hnt, st, h, u_bf, vt_bf)


def _bf16_round(x):
    return x.astype(BF16).astype(F32)


def _attend_rows(q, kb, vb):
    s = jnp.sum(_bf16_round(kb) * _bf16_round(q)[None], axis=2, keepdims=True)
    p = jnp.exp(s - jnp.max(s, axis=0, keepdims=True))
    p = p / jnp.sum(p, axis=0, keepdims=True)
    return jnp.sum(_bf16_round(p) * _bf16_round(vb), axis=0)


def _mem_sample_kernel(qm_ref, mk_ref, mv_ref, o_ref):
    o_ref[0] = _attend_rows(qm_ref[0] * (MEM_HD ** -0.5), mk_ref[0], mv_ref[0])


def mem_attend_sample(qm, cache_mem_k, cache_mem_v):
    b = qm.shape[0]
    blk = pl.BlockSpec((1, MEM_HEADS, MEM_HD), lambda i: (i, 0, 0))
    cache = pl.BlockSpec((1, MEM_TOKENS, MEM_HEADS, MEM_HD), lambda i: (i, 0, 0, 0))
    return pl.pallas_call(
        _mem_sample_kernel, grid=(b,), in_specs=[blk, cache, cache], out_specs=blk,
        out_shape=jax.ShapeDtypeStruct((b, MEM_HEADS, MEM_HD), F32),
        compiler_params=_params("parallel"),
    )(qm.reshape(b, MEM_HEADS, MEM_HD), cache_mem_k, cache_mem_v)


PAGE_GROUP = 8
RING_SLOTS = 8
SCORE_ROWS = 256


def _select_tile(keys, topk):
    rows = keys.shape[0]

    def count_ge(cand):
        return jnp.sum(jnp.sum(jnp.where(keys >= cand, 1.0, 0.0), axis=0, keepdims=True), axis=1, keepdims=True)

    n_valid = count_ge(jnp.full((1, 1), INT_MIN + 1, I32))
    thr = jnp.maximum(_kth_largest_key(count_ge, n_valid, float(topk)), INT_MIN + 1)
    need = float(topk) - count_ge(thr + 1)
    tie = jnp.where(keys == thr, 1.0, 0.0)
    ri = lax.broadcasted_iota(I32, (LANES, LANES), 0)
    ci = lax.broadcasted_iota(I32, (LANES, LANES), 1)
    within = _dot(tie.astype(BF16), jnp.where(ri <= ci, 1.0, 0.0).astype(BF16))
    row_tot = jnp.broadcast_to(jnp.sum(tie, axis=1, keepdims=True), (rows, LANES))
    rr = lax.broadcasted_iota(I32, (rows, rows), 0)
    rc = lax.broadcasted_iota(I32, (rows, rows), 1)
    before = _dot(jnp.where(rc < rr, 1.0, 0.0).astype(BF16), row_tot.astype(BF16))
    take_tie = jnp.where(within + before <= need, tie, 0.0)
    return jnp.where(keys > thr, 1.0, take_tie)


def _sample_attn_kernel(pt_ref, q_ref, qi_ref, wi_ref, kin_ref, kn_ref, vn_ref, cki_ref, ck_ref, cv_ref,
                        o_ref, kibuf, ring, qb_ref, key_ref, bias_ref, satt_ref, acc_ref,
                        sem_ki, sem_ring, *, topk):
    b, nb = pl.program_id(0), pl.num_programs(0)
    n_pages = pt_ref.shape[1]
    groups = n_pages // PAGE_GROUP
    n_stream = nb * 2 * groups
    par = b % 2

    def ki_copy(bb, buf, j):
        return pltpu.make_async_copy(cki_ref.at[pt_ref[bb, j]], kibuf.at[buf, j], sem_ki.at[buf])

    def start_ki(bb, buf):
        lax.fori_loop(0, n_pages, lambda j, c: (ki_copy(bb, buf, j).start(), c)[1], 0)

    def group_copy(cache_ref, bb, g, slot, i):
        return pltpu.make_async_copy(cache_ref.at[pt_ref[bb, g * PAGE_GROUP + i]], ring.at[slot, i],
                                     sem_ring.at[slot])

    def start_group(n):
        bb, g, slot = n // (2 * groups), n % (2 * groups), n % RING_SLOTS

        @pl.when(g < groups)
        def _():
            for i in range(PAGE_GROUP):
                group_copy(ck_ref, bb, g, slot, i).start()

        @pl.when(g >= groups)
        def _():
            for i in range(PAGE_GROUP):
                group_copy(cv_ref, bb, g - groups, slot, i).start()

    def next_group(n):
        slot = n % RING_SLOTS
        for i in range(PAGE_GROUP):
            group_copy(ck_ref, 0, 0, slot, i).wait()

        @pl.when(n + RING_SLOTS - 1 < n_stream)
        def _():
            start_group(n + RING_SLOTS - 1)

        return slot

    @pl.when(b == 0)
    def _():
        start_ki(0, 0)
        for n in range(RING_SLOTS - 1):
            start_group(n)

    @pl.when(b + 1 < nb)
    def _():
        start_ki(b + 1, 1 - par)

    lax.fori_loop(0, n_pages, lambda j, c: (ki_copy(b, par, j).wait(), c)[1], 0)

    lanes = (IDX_DIM, LANES)
    for h in range(ATT_HEADS):
        qb_ref[h] = jnp.broadcast_to(q_ref[0, h], lanes)
    qi_bf = qi_ref[0].astype(BF16)
    w = wi_ref[0]
    lane = lax.broadcasted_iota(I32, (1, LANES), 1)

    def head_dots(page_of_head, qsrc):
        return jnp.concatenate([jnp.sum(page_of_head(h) * qsrc[h], axis=0, keepdims=True)
                                for h in range(ATT_HEADS)], axis=0)

    def index_row(page):
        s = _dot(qi_bf, page.astype(BF16))
        return _sortable(jnp.sum(w * jnp.maximum(s, 0.0), axis=0, keepdims=True))

    key_ref[...] = jnp.full(key_ref.shape, INT_MIN, I32)

    def score_pages(g, c):
        j0 = pl.multiple_of(g * SUBLANES, SUBLANES)
        rows = [index_row(kibuf[par, j0 + i]) for i in range(SUBLANES)]
        key_ref[pl.ds(j0, SUBLANES), :] = jnp.concatenate(rows, axis=0)
        return c

    lax.fori_loop(0, n_pages // SUBLANES, score_pages, 0)
    new_row = index_row(jnp.broadcast_to(kin_ref[0], lanes))
    key_ref[pl.ds(n_pages, 1), :] = jnp.where(lane == 0, new_row, INT_MIN)
    bias_ref[...] = jnp.where(_select_tile(key_ref[...], topk) > 0.0, 0.0, MASKED_SCORE)

    def att_row(page, j):
        satt_ref[j] = head_dots(lambda h: page[h], qb_ref) + bias_ref[pl.ds(j, 1), :]

    def k_group(g, c):
        slot = next_group(b * 2 * groups + g)
        for i in range(PAGE_GROUP):
            att_row(ring[slot, i], g * PAGE_GROUP + i)
        return c

    lax.fori_loop(0, groups, k_group, 0)
    k_new = [jnp.broadcast_to(kn_ref[0, h], lanes) for h in range(ATT_HEADS)]
    att_row(k_new, n_pages)

    m = lax.fori_loop(0, n_pages + 1, lambda j, m: jnp.maximum(m, satt_ref[j]),
                      jnp.full((ATT_HEADS, LANES), MASKED_SCORE, F32))
    m = jnp.max(m, axis=1, keepdims=True)

    acc_ref[...] = jnp.zeros(acc_ref.shape, F32)

    def add_page(page, j, psum):
        p = jnp.exp(satt_ref[j] - m)
        for h in range(ATT_HEADS):
            acc_ref[h] += page[h] * p[h:h + 1, :]
        return psum + p

    def v_group(g, psum):
        slot = next_group(b * 2 * groups + groups + g)
        for i in range(PAGE_GROUP):
            psum = add_page(ring[slot, i], g * PAGE_GROUP + i, psum)
        return psum

    psum = lax.fori_loop(0, groups, v_group, jnp.zeros((ATT_HEADS, LANES), F32))
    v_new = [jnp.broadcast_to(vn_ref[0, h], lanes) for h in range(ATT_HEADS)]
    psum = add_page(v_new, n_pages, psum)
    denom = jnp.sum(psum, axis=1, keepdims=True)
    for h in range(ATT_HEADS):
        o_ref[0, h] = jnp.sum(acc_ref[h], axis=1, keepdims=True) / denom[h:h + 1, :]


def sample_attention(q, qi, wi, ki_new, k_new, v_new, cache_idx_k, cache_k, cache_v, page_table, topk):
    b = q.shape[0]
    n_pages = page_table.shape[1]
    col = lambda a, h: a.reshape(b, h, -1, 1)
    cki = jnp.transpose(cache_idx_k, (0, 2, 1))
    ck = jnp.transpose(cache_k, (0, 2, 3, 1))
    cv = jnp.transpose(cache_v, (0, 2, 3, 1))
    head_col = pl.BlockSpec((1, ATT_HEADS, HEAD_DIM, 1), lambda i, pt: (i, 0, 0, 0))
    any_spec = pl.BlockSpec(memory_space=pl.ANY)
    grid_spec = pltpu.PrefetchScalarGridSpec(
        num_scalar_prefetch=1, grid=(b,),
        in_specs=[head_col, pl.BlockSpec((1, IDX_HEADS, IDX_DIM), lambda i, pt: (i, 0, 0)),
                  pl.BlockSpec((1, IDX_HEADS, 1), lambda i, pt: (i, 0, 0)),
                  pl.BlockSpec((1, IDX_DIM, 1), lambda i, pt: (i, 0, 0)),
                  head_col, head_col, any_spec, any_spec, any_spec],
        out_specs=head_col,
        scratch_shapes=[pltpu.VMEM((2, n_pages, IDX_DIM, PAGE_SIZE), F32),
                        pltpu.VMEM((RING_SLOTS, PAGE_GROUP, ATT_HEADS, HEAD_DIM, PAGE_SIZE), F32),
                        pltpu.VMEM((ATT_HEADS, HEAD_DIM, LANES), F32),
                        pltpu.VMEM((SCORE_ROWS, LANES), I32),
                        pltpu.VMEM((SCORE_ROWS, LANES), F32),
                        pltpu.VMEM((n_pages + 1, ATT_HEADS, LANES), F32),
                        pltpu.VMEM((ATT_HEADS, HEAD_DIM, LANES), F32),
                        pltpu.SemaphoreType.DMA((2,)), pltpu.SemaphoreType.DMA((RING_SLOTS,))])
    out = pl.pallas_call(
        functools.partial(_sample_attn_kernel, topk=topk), grid_spec=grid_spec,
        out_shape=jax.ShapeDtypeStruct((b, ATT_HEADS, HEAD_DIM, 1), F32),
        compiler_params=_params("arbitrary"),
    )(page_table, col(q, ATT_HEADS), qi.reshape(b, IDX_HEADS, IDX_DIM), wi[:, :IDX_HEADS].reshape(b, IDX_HEADS, 1),
      ki_new.reshape(b, IDX_DIM, 1), col(k_new, ATT_HEADS), col(v_new, ATT_HEADS), cki, ck, cv)
    return out.reshape(b, ATT_W)


def _pad_rows(a, rows):
    return jnp.pad(a, ((0, rows - a.shape[0]), (0, 0)))


def kernel(x_prompt, x_sample, cache_k, cache_v, cache_idx_k, cache_mem_k, cache_mem_v, state_ssm_re, state_ssm_im, page_table, mem_prompt, norm_mix, w_in, q_norm, k_norm, idx_k_norm, mq_norm, ssm_a_re, ssm_a_im, ssm_b_re, ssm_b_im, ssm_c_re, ssm_c_im, ssm_d, ssm_log_dt, w_glu, mem_norm, w_mem_kv, mk_norm, w_br_attn, w_br_ssm, w_br_mem, w_out, norm_ffn, peer_w_q, peer_sub_keys, peer_u, peer_v):
    seq = x_prompt.shape[1]
    bd = x_sample.shape[0]
    w_parts = _split_w_in(w_in)
    tables = ssm_tables(ssm_a_re, ssm_a_im, ssm_b_re, ssm_b_im, ssm_c_re, ssm_c_im, ssm_log_dt)
    u_bf = peer_u.astype(BF16)
    vt_bf = peer_v.T.astype(BF16)
    tail = (w_br_attn, w_br_ssm, w_br_mem, w_out, norm_ffn, peer_w_q, peer_sub_keys)
    norms = (q_norm, k_norm, idx_k_norm, mq_norm)

    mk, mv = mem_kv(mem_prompt[0], mem_norm, w_mem_kv, mk_norm)
    (k_p, v_p, ki_p, wi, u, gate, q_ext, qi_ext, k_bf, v_bf, ki_dup, mem_o) = project(
        x_prompt[0], jnp.arange(seq), norm_mix, w_parts, *norms, mem=(mk, mv))
    att_o = attn_prompt(qi_ext, q_ext, wi, ki_dup, k_bf, v_bf)
    ssm_o, hre_p, him_p = ssm_prompt(u, tables, ssm_d, w_glu)
    h, hn, st = merge(x_prompt[0], gate, att_o, ssm_o, mem_o, *tail)
    y_prompt = peer(hn, st, h, u_bf, vt_bf)

    (k_s, v_s, ki_s, wi, u, gate, q, qi, qm) = project(
        x_sample[:, 0], jnp.full((bd,), PAST_LEN), norm_mix, w_parts, *norms)
    att_o = sample_attention(q, qi, wi, ki_s, k_s, v_s, cache_idx_k, cache_k, cache_v, page_table,
                             min(IDX_TOPK_MAX, (PAST_LEN + 1) // 4))
    ssm_o, hre_s, him_s = ssm_sample(u, state_ssm_re, state_ssm_im, tables, ssm_d, w_glu)
    mem_o = mem_attend_sample(qm, cache_mem_k, cache_mem_v).reshape(bd, MEM_W)
    pad = lambda a: _pad_rows(a, PEER_TOKENS)
    h, hn, st = merge(pad(x_sample[:, 0]), pad(gate), pad(att_o), pad(ssm_o), pad(mem_o), *tail)
    y_sample = peer(hn, st, h, u_bf, vt_bf)[:bd]

    heads = lambda a: a.reshape(a.shape[0], ATT_HEADS, HEAD_DIM)
    state = lambda a, n: a.reshape(n, SSM_GROUPS, SSM_STATE)
    return (y_prompt[None], y_sample[:, None],
            heads(k_p)[None], heads(v_p)[None], ki_p[None],
            mk.reshape(1, MEM_TOKENS, MEM_HEADS, MEM_HD), mv.reshape(1, MEM_TOKENS, MEM_HEADS, MEM_HD),
            state(hre_p, 1), state(him_p, 1),
            heads(k_s)[:, None], heads(v_s)[:, None], ki_s[:, None],
            state(hre_s, bd), state(him_s, bd))
```

```python
import functools
import math

import jax
import jax.numpy as jnp
import numpy as np
from jax import lax
from jax.experimental import pallas as pl
from jax.experimental.pallas import tpu as pltpu

F32 = jnp.float32
BF16 = jnp.bfloat16
I32 = jnp.int32

D_MODEL = 1024
PAST_LEN = 16384
PAGE_SIZE = 128
ATT_HEADS = 8
HEAD_DIM = 64
ATT_W = ATT_HEADS * HEAD_DIM
IDX_HEADS = 8
IDX_DIM = 64
IDX_TOPK_MAX = 256
ROPE_THETA = 500000.0
ROPE_HALF = HEAD_DIM // 4 // 2
SSM_GROUP = 16
SSM_W = 512
SSM_GROUPS = SSM_W // SSM_GROUP
SSM_STATE = 64
SSM_LANES = SSM_GROUPS * SSM_STATE
MEM_TOKENS = 256
MEM_HEADS = 4
MEM_HD = 128
MEM_W = MEM_HEADS * MEM_HD
N_BRANCH = 3
PEER_HEADS = 8
N_KEYS = 128
N_EXPERTS = N_KEYS * N_KEYS
PEER_DK = 128
PEER_TOPK = 16
EPS = 1e-6

LANES = 128
SUBLANES = 8
VMEM_LIMIT = 56 * 1024 * 1024
INT_MIN = -(2 ** 31)
MASKED_SCORE = -1e30
RUNNING_MAX_INIT = -5e29

NT_DIMS = (((1,), (1,)), ((), ()))
LOG2_E = math.log2(math.e)


def _params(*sem):
    return pltpu.CompilerParams(dimension_semantics=sem, vmem_limit_bytes=VMEM_LIMIT)


def _full(shape):
    nd = len(shape)
    return pl.BlockSpec(shape, lambda *_: (0,) * nd, pipeline_mode=pl.Buffered(1))


def _rms(x, g, n):
    ms = jnp.sum(x * x, axis=-1, keepdims=True) * (1.0 / n)
    return x * lax.rsqrt(ms + EPS) * g


def _dot(a, b):
    return jnp.dot(a, b, preferred_element_type=F32)


def _dot_nt(a, b):
    return lax.dot_general(a, b, NT_DIMS, preferred_element_type=F32)


def _mem_kv_kernel(mem_ref, mn_ref, w_ref, mkn_ref, mk_ref, mv_ref):
    xn = _rms(mem_ref[...], mn_ref[...], D_MODEL).astype(BF16)
    y = _dot(xn, w_ref[...])
    for h in range(MEM_HEADS):
        sl = slice(h * MEM_HD, (h + 1) * MEM_HD)
        mk_ref[:, sl] = _rms(y[:, sl], mkn_ref[...], MEM_HD)
    mv_ref[...] = y[:, MEM_W:]


def mem_kv(mem, mem_norm, w_mem_kv, mk_norm):
    m = mem.shape[0]
    return pl.pallas_call(
        _mem_kv_kernel,
        out_shape=(jax.ShapeDtypeStruct((m, MEM_W), F32), jax.ShapeDtypeStruct((m, MEM_W), F32)),
        compiler_params=pltpu.CompilerParams(vmem_limit_bytes=VMEM_LIMIT),
    )(mem, mem_norm.reshape(1, -1), w_mem_kv.astype(BF16), mk_norm.reshape(1, -1))


def _rope_tables(pos):
    inv_freq = ROPE_THETA ** (-jnp.arange(ROPE_HALF, dtype=F32) / ROPE_HALF)
    ang = pos.astype(F32)[:, None] * inv_freq[None, :]
    cos, sin = jnp.cos(ang), jnp.sin(ang)
    t = pos.shape[0]
    rest = HEAD_DIM - 2 * ROPE_HALF
    c = jnp.concatenate([cos, cos, jnp.ones((t, rest), F32)], axis=1)
    s_hi = jnp.concatenate([-sin, jnp.zeros((t, HEAD_DIM - ROPE_HALF), F32)], axis=1)
    s_lo = jnp.concatenate([jnp.zeros((t, ROPE_HALF), F32), sin, jnp.zeros((t, rest), F32)], axis=1)
    rep = LANES // HEAD_DIM
    return jnp.tile(c, (1, rep)), jnp.tile(s_hi, (1, rep)), jnp.tile(s_lo, (1, rep))


def _rope(y, c, s_hi, s_lo):
    w = y.shape[1]
    rep = w // LANES
    if rep > 1:
        c = jnp.concatenate([c] * rep, axis=1)
        s_hi = jnp.concatenate([s_hi] * rep, axis=1)
        s_lo = jnp.concatenate([s_lo] * rep, axis=1)
    return y * c + pltpu.roll(y, w - ROPE_HALF, axis=1) * s_hi + pltpu.roll(y, ROPE_HALF, axis=1) * s_lo


def _group_rms(y, grp, g, n):
    sq = y * y
    hi = sq.astype(BF16)
    lo = (sq - hi.astype(F32)).astype(BF16)
    ss = _dot(hi, grp) + _dot(lo, grp)
    return y * lax.rsqrt(ss * (1.0 / n) + EPS) * g


def _proj_kernel(*refs, prompt):
    (x_ref, nm_ref, wq_ref, wk_ref, wv_ref, wqi_ref, wki_ref, wwi_ref, wu_ref, wqm_ref, wg_ref,
     qn_ref, kn_ref, ikn_ref, mqn_ref, grp_ref, c_ref, shi_ref, slo_ref) = refs[:19]
    refs = refs[19:]
    if prompt:
        mk_ref, mv_ref = refs[:2]
        (k_out, v_out, ki_out, wi_out, u_out, gate_out,
         qx_out, qix_out, kbf_out, vbf_out, kid_out, memo_out) = refs[2:]
    else:
        k_out, v_out, ki_out, wi_out, u_out, gate_out, q_out, qi_out, qm_out = refs

    xn = _rms(x_ref[...], nm_ref[...], D_MODEL).astype(BF16)
    c, s_hi, s_lo = c_ref[...], shi_ref[...], slo_ref[...]
    grp = grp_ref[...]

    q = _rope(_group_rms(_dot(xn, wq_ref[...]), grp, qn_ref[...], HEAD_DIM), c, s_hi, s_lo)
    q = q * (HEAD_DIM ** -0.5)
    k = _rope(_group_rms(_dot(xn, wk_ref[...]), grp, kn_ref[...], HEAD_DIM), c, s_hi, s_lo)
    v = _dot(xn, wv_ref[...])
    qi = _rope(_dot(xn, wqi_ref[...]), c, s_hi, s_lo)
    ki2 = _dot(xn, wki_ref[...])
    ki2 = _rope(_rms(ki2, ikn_ref[...], 2 * IDX_DIM), c, s_hi, s_lo)
    wi = _dot(xn, wwi_ref[...]) * (IDX_HEADS ** -0.5 * IDX_DIM ** -0.5)
    qm = _dot(xn, wqm_ref[...])
    qm = jnp.concatenate(
        [_rms(qm[:, h * MEM_HD:(h + 1) * MEM_HD], mqn_ref[...], MEM_HD) for h in range(MEM_HEADS)], axis=1)

    k_out[...] = k
    v_out[...] = v
    ki_out[...] = ki2[:, :IDX_DIM]
    wi_out[...] = wi
    u_out[...] = _dot(xn, wu_ref[...])
    gate_out[...] = jax.nn.sigmoid(_dot(xn, wg_ref[...]))

    if not prompt:
        q_out[...] = q
        qi_out[...] = qi
        qm_out[...] = qm
        return

    kbf_out[...] = k.astype(BF16)
    vbf_out[...] = v.astype(BF16)
    kid_out[...] = ki2.astype(BF16)
    lane = lax.broadcasted_iota(I32, (q.shape[0], LANES), 1)
    for h in range(ATT_HEADS):
        pair = slice((h // 2) * LANES, (h // 2 + 1) * LANES)
        own = (lane < HEAD_DIM) if h % 2 == 0 else (lane >= HEAD_DIM)
        qx_out[h] = jnp.where(own, q[:, pair] * LOG2_E, 0.0).astype(BF16)
        qix_out[h] = jnp.where(own, qi[:, pair], 0.0).astype(BF16)

    for h in range(MEM_HEADS):
        sl = slice(h * MEM_HD, (h + 1) * MEM_HD)
        s = _dot_nt(qm[:, sl].astype(BF16), mk_ref[:, sl]) * (MEM_HD ** -0.5)
        p = jnp.exp(s - jnp.max(s, axis=-1, keepdims=True))
        p = p / jnp.sum(p, axis=-1, keepdims=True)
        memo_out[:, sl] = _dot(p.astype(BF16), mv_ref[:, sl])


def _split_w_in(w_in):
    splits = np.cumsum([ATT_W, ATT_W, ATT_W, IDX_HEADS * IDX_DIM, IDX_DIM, IDX_HEADS, SSM_W, MEM_W])
    wq, wk, wv, wqi, wki, wwi, wu, wqm, wg = jnp.split(w_in.astype(BF16), splits.tolist(), axis=1)
    wki = jnp.concatenate([wki, wki], axis=1)
    wwi = jnp.pad(wwi, ((0, 0), (0, LANES - IDX_HEADS)))
    return wq, wk, wv, wqi, wki, wwi, wu, wqm, wg


def project(x, pos, norm_mix, w_parts, q_norm, k_norm, idx_k_norm, mq_norm, mem=None, tile=256):
    t = x.shape[0]
    tile = min(tile, t)
    prompt = mem is not None
    heads_per_slab = ATT_W // HEAD_DIM
    grp = jnp.kron(jnp.eye(heads_per_slab, dtype=F32), jnp.ones((HEAD_DIM, HEAD_DIM), F32)).astype(BF16)
    c, s_hi, s_lo = _rope_tables(pos)
    tile_rep = lambda g, n: jnp.tile(g.reshape(1, -1), (1, n))
    ins = [x, norm_mix.reshape(1, -1), *w_parts,
           tile_rep(q_norm, ATT_HEADS), tile_rep(k_norm, ATT_HEADS), tile_rep(idx_k_norm, 2),
           mq_norm.reshape(1, -1), grp, c, s_hi, s_lo]
    row = lambda w: pl.BlockSpec((tile, w), lambda i: (i, 0))
    in_specs = [row(D_MODEL)] + [_full(a.shape) for a in ins[1:16]] + [row(LANES)] * 3
    outs = [(ATT_W, F32), (ATT_W, F32), (IDX_DIM, F32), (LANES, F32), (SSM_W, F32), (N_BRANCH * D_MODEL, F32)]
    out_shape = [jax.ShapeDtypeStruct((t, w), d) for w, d in outs]
    out_specs = [row(w) for w, _ in outs]
    if prompt:
        mk, mv = mem
        ins += [mk.astype(BF16), mv.astype(BF16)]
        in_specs += [_full(mk.shape), _full(mv.shape)]
        ext = pl.BlockSpec((ATT_HEADS, tile, LANES), lambda i: (0, i, 0))
        out_shape += [jax.ShapeDtypeStruct((ATT_HEADS, t, LANES), BF16)] * 2
        out_specs += [ext, ext]
        for w, d in [(ATT_W, BF16), (ATT_W, BF16), (LANES, BF16), (MEM_W, F32)]:
            out_shape.append(jax.ShapeDtypeStruct((t, w), d))
            out_specs.append(row(w))
    else:
        for w in (ATT_W, IDX_HEADS * IDX_DIM, MEM_W):
            out_shape.append(jax.ShapeDtypeStruct((t, w), F32))
            out_specs.append(row(w))
    return pl.pallas_call(
        functools.partial(_proj_kernel, prompt=prompt),
        grid=(t // tile,),
        in_specs=in_specs, out_specs=out_specs, out_shape=out_shape,
        compiler_params=_params("parallel"),
    )(*ins)


def _sortable(x):
    b = pltpu.bitcast(jnp.where(x == 0.0, 0.0, x), I32)
    return jnp.where(b < 0, b ^ 0x7FFFFFFF, b)


def _kth_largest_key(count_ge, n_valid, k):
    c0 = count_ge(jnp.zeros(n_valid.shape, I32))
    pos = c0 >= k
    ans, cnt = jnp.where(pos, 0, INT_MIN).astype(I32), jnp.where(pos, c0, n_valid)

    def pending(cnt):
        return jnp.max(jnp.where((cnt != k) & (n_valid >= k), 1.0, 0.0))

    def body(state):
        t, _, ans, cnt = state
        cand = ans | jnp.left_shift(jnp.int32(1), 30 - t)
        c = count_ge(cand)
        up = c >= k
        ans, cnt = jnp.where(up, cand, ans), jnp.where(up, c, cnt)
        return t + 1, pending(cnt), ans, cnt

    state = (jnp.int32(0), pending(cnt), ans, cnt)
    return lax.while_loop(lambda s: (s[0] < 31) & (s[1] > 0.0), body, state)[2]


def _attn_prompt_kernel(qi_ref, q_ref, wi_ref, kid_ref, k_ref, v_ref, o_ref,
                        s_ref, m_ref, l_ref, acc_ref, bias_ref, qk_ref, p_even, p_odd, *, tq, tk, topk):
    q0 = pl.program_id(0) * tq
    nkc = (q0 + tq + tk - 1) // tk
    row = q0 + lax.broadcasted_iota(I32, (tq, tk), 0)
    col0 = lax.broadcasted_iota(I32, (tq, tk), 1)
    wi = wi_ref[...]

    def score_chunk(kc, carry):
        off = pl.multiple_of(kc * tk, tk)
        kd = kid_ref[pl.ds(off, tk), :]
        acc = jnp.zeros((tq, tk), F32)
        for h in range(IDX_HEADS):
            acc = acc + wi[:, h:h + 1] * jnp.maximum(_dot_nt(qi_ref[h], kd), 0.0)
        s_ref[kc] = jnp.where(col0 + off <= row, _sortable(acc), INT_MIN)
        return carry

    lax.fori_loop(0, nkc, score_chunk, 0)

    def count_ge(cand):
        candb = jnp.broadcast_to(cand, (tq, LANES))

        def body(kc, cnt):
            blk = s_ref[kc]
            for j in range(tk // LANES):
                cnt = cnt + jnp.where(blk[:, j * LANES:(j + 1) * LANES] >= candb, 1.0, 0.0)
            return cnt

        cnt = lax.fori_loop(0, nkc, body, jnp.zeros((tq, LANES), F32))
        return jnp.sum(cnt, axis=1, keepdims=True)

    n_valid = (q0 + 1 + lax.broadcasted_iota(I32, (tq, 1), 0)).astype(F32)
    thr = jnp.maximum(_kth_largest_key(count_ge, n_valid, float(topk)), INT_MIN + 1)

    m_ref[...] = jnp.full(m_ref.shape, RUNNING_MAX_INIT, F32)
    l_ref[...] = jnp.zeros(l_ref.shape, F32)
    acc_ref[...] = jnp.zeros(acc_ref.shape, F32)

    cols = [slice(j * LANES, (j + 1) * LANES) for j in range(tk // LANES)]

    pairs = [slice((h // 2) * LANES, (h // 2 + 1) * LANES) for h in range(ATT_HEADS)]

    def apply_values(p_ref, kc):
        off = pl.multiple_of(kc * tk, tk)
        for h in range(ATT_HEADS):
            acc_ref[h] += _dot(p_ref[h], v_ref[pl.ds(off, tk), pairs[h]])

    def chunk_step(kc, p_new, p_prev):
        off = pl.multiple_of(kc * tk, tk)
        bias_ref[...] = jnp.where(s_ref[kc] >= thr, 0.0, MASKED_SCORE)
        off_prev = pl.multiple_of(jnp.maximum(kc - 1, 0) * tk, tk)
        for h in range(ATT_HEADS):
            qk_ref[h] = _dot_nt(q_ref[h], k_ref[pl.ds(off, tk), pairs[h]]) + bias_ref[...]
            acc_ref[h] += _dot(p_prev[h], v_ref[pl.ds(off_prev, tk), pairs[h]])
        for h in range(ATT_HEADS):
            mx = functools.reduce(jnp.maximum, [qk_ref[h, :, c] for c in cols])
            m_old = m_ref[h]
            m_new = jnp.maximum(m_old, jnp.max(mx, axis=1, keepdims=True))
            alpha = jnp.exp2(m_old - m_new)
            psum = jnp.zeros((tq, LANES), F32)
            for c in cols:
                p = jnp.exp2(qk_ref[h, :, c] - m_new)
                psum = psum + p
                p_new[h, :, c] = p.astype(BF16)
            m_ref[h] = m_new
            l_ref[h] = alpha * l_ref[h] + psum
            acc_ref[h] = alpha * acc_ref[h]

    p_odd[...] = jnp.zeros(p_odd.shape, BF16)

    def attend_chunk(kc, carry):
        @pl.when(kc % 2 == 0)
        def _():
            chunk_step(kc, p_even, p_odd)

        @pl.when(kc % 2 == 1)
        def _():
            chunk_step(kc, p_odd, p_even)

        return carry

    lax.fori_loop(0, nkc, attend_chunk, 0)

    @pl.when(nkc % 2 == 1)
    def _():
        apply_values(p_even, nkc - 1)

    @pl.when(nkc % 2 == 0)
    def _():
        apply_values(p_odd, nkc - 1)

    lane = lax.broadcasted_iota(I32, (tq, LANES), 1)
    for j in range(ATT_HEADS // 2):
        even = acc_ref[2 * j] / jnp.sum(l_ref[2 * j], axis=1, keepdims=True)
        odd = acc_ref[2 * j + 1] / jnp.sum(l_ref[2 * j + 1], axis=1, keepdims=True)
        o_ref[:, j * LANES:(j + 1) * LANES] = jnp.where(lane < HEAD_DIM, even, odd)


def attn_prompt(qi_ext, q_ext, wi, ki_dup, k_bf, v_bf, tq=128, tk=512):
    t = k_bf.shape[0]
    tq, tk = min(tq, t), min(tk, t)
    topk = min(IDX_TOPK_MAX, t // 4)
    ext = pl.BlockSpec((ATT_HEADS, tq, LANES), lambda i: (0, i, 0))
    return pl.pallas_call(
        functools.partial(_attn_prompt_kernel, tq=tq, tk=tk, topk=topk),
        grid=(t // tq,),
        in_specs=[ext, ext, pl.BlockSpec((tq, LANES), lambda i: (i, 0)),
                  _full(ki_dup.shape), _full(k_bf.shape), _full(v_bf.shape)],
        out_specs=pl.BlockSpec((tq, ATT_W), lambda i: (i, 0)),
        out_shape=jax.ShapeDtypeStruct((t, ATT_W), F32),
        scratch_shapes=[pltpu.VMEM((t // tk, tq, tk), I32),
                        pltpu.VMEM((ATT_HEADS, tq, LANES), F32),
                        pltpu.VMEM((ATT_HEADS, tq, LANES), F32),
                        pltpu.VMEM((ATT_HEADS, tq, LANES), F32),
                        pltpu.VMEM((tq, tk), F32),
                        pltpu.VMEM((ATT_HEADS, tq, tk), F32),
                        pltpu.VMEM((ATT_HEADS, tq, tk), BF16),
                        pltpu.VMEM((ATT_HEADS, tq, tk), BF16)],
        compiler_params=_params("arbitrary"),
    )(qi_ext, q_ext, wi, ki_dup, k_bf, v_bf)


SSM_SEG = 32
SSM_CHUNK = SUBLANES * SSM_SEG


def _cexp(re, im):
    mag = jnp.exp(re)
    return mag * jnp.cos(im), mag * jnp.sin(im)


def _ssm_prep_kernel(are_ref, aim_ref, ldt_ref, are16_ref, aim16_ref, bre_ref, bim_ref,
                     bbre_ref, bbim_ref, pre_ref, pim_ref):
    dt = jnp.exp(ldt_ref[...])
    ar, ai = are16_ref[...], aim16_ref[...]
    er, ei = _cexp(ar * dt, ai * dt)
    nr, ni = er - 1.0, ei
    den = ar * ar + ai * ai
    cr, ci = (nr * ar + ni * ai) / den, (ni * ar - nr * ai) / den
    br, bi = bre_ref[...], bim_ref[...]
    bbre_ref[...] = cr * br - ci * bi
    bbim_ref[...] = cr * bi + ci * br
    ar, ai = are_ref[...] * dt, aim_ref[...] * dt
    for j in range(SSM_SEG):
        pr, pi = _cexp(ar * (j + 1.0), ai * (j + 1.0))
        pre_ref[j] = pr
        pim_ref[j] = pi


def ssm_tables(a_re, a_im, b_re, b_im, c_re, c_im, log_dt):
    g, p, c = SSM_GROUPS, SSM_STATE, SSM_GROUP
    rep = lambda a: jnp.repeat(a, c, axis=1)
    out_shape = ([jax.ShapeDtypeStruct((g, p * c), F32)] * 2
                 + [jax.ShapeDtypeStruct((SSM_SEG, g, p), F32)] * 2)
    bbre, bbim, pre, pim = pl.pallas_call(
        _ssm_prep_kernel, out_shape=out_shape,
        compiler_params=pltpu.CompilerParams(vmem_limit_bytes=VMEM_LIMIT),
    )(a_re, a_im, log_dt.reshape(g, 1), rep(a_re), rep(a_im), b_re.reshape(g, p * c), b_im.reshape(g, p * c))
    eye = jnp.eye(g, dtype=F32)
    blockdiag_in = lambda bb: jnp.einsum('gpc,gh->gchp', bb.reshape(g, p, c), eye).reshape(g * c, g * p)
    blockdiag_out = lambda cc: jnp.einsum('gcp,gh->gphc', cc, eye).reshape(g * p, g * c)
    bmat = jnp.concatenate([blockdiag_in(bbre), blockdiag_in(bbim)], axis=1)
    cmat = jnp.concatenate([blockdiag_out(c_re), blockdiag_out(-c_im)], axis=0)
    return bmat, cmat, pre.reshape(SSM_SEG, g * p), pim.reshape(SSM_SEG, g * p)


def _ssm_out(h_all, u, cmat_ref, d_ref, wglu_ref):
    y = _dot(h_all.astype(BF16), cmat_ref[...]) + d_ref[...] * u
    z = jax.nn.gelu(y)
    return z * jax.nn.sigmoid(_dot(z.astype(BF16), wglu_ref[...]))


def _ssm_prompt_kernel(u_ref, bmat_ref, cmat_ref, pre_ref, pim_ref, d_ref, wglu_ref,
                       o_ref, hre_ref, him_ref, bu_ref, h_ref, carry_ref):
    n = SSM_LANES

    @pl.when(pl.program_id(0) == 0)
    def _():
        carry_ref[...] = jnp.zeros(carry_ref.shape, F32)

    u = u_ref[...]
    bu = _dot(u.astype(BF16), bmat_ref[...])
    n_blk = n // LANES
    for b in range(2 * n_blk):
        bu_ref[b] = bu[:, b * LANES:(b + 1) * LANES]

    group = 8
    for b0 in range(0, n_blk, group):
        a = [(jnp.broadcast_to(pre_ref[0:1, pl.ds(b * LANES, LANES)], (SUBLANES, LANES)),
              jnp.broadcast_to(pim_ref[0:1, pl.ds(b * LANES, LANES)], (SUBLANES, LANES)))
             for b in range(b0, b0 + group)]

        def step(j, carry):
            rows = pl.ds(pl.multiple_of(j * SUBLANES, SUBLANES), SUBLANES)
            out = []
            for g in range(group):
                (a_re, a_im), (h_re, h_im) = a[g], carry[g]
                n_re = a_re * h_re - a_im * h_im + bu_ref[b0 + g, rows, :]
                n_im = a_re * h_im + a_im * h_re + bu_ref[n_blk + b0 + g, rows, :]
                h_ref[b0 + g, rows, :] = n_re
                h_ref[n_blk + b0 + g, rows, :] = n_im
                out.append((n_re, n_im))
            return tuple(out)

        zero = jnp.zeros((SUBLANES, LANES), F32)
        lax.fori_loop(0, SSM_SEG, step, ((zero, zero),) * group)

    last = pl.ds((SSM_SEG - 1) * SUBLANES, SUBLANES)
    for b in range(n_blk):
        re_sl, im_sl = pl.ds(b * LANES, LANES), pl.ds(n + b * LANES, LANES)
        e_re, e_im = h_ref[b, last, :], h_ref[n_blk + b, last, :]
        w_re, w_im = pre_ref[SSM_SEG - 1:SSM_SEG, re_sl], pim_ref[SSM_SEG - 1:SSM_SEG, re_sl]
        f_re, f_im = carry_ref[:, re_sl], carry_ref[:, im_sl]
        enter_re, enter_im = [], []
        for s in range(SUBLANES):
            enter_re.append(f_re)
            enter_im.append(f_im)
            f_re, f_im = (e_re[s:s + 1] + w_re * f_re - w_im * f_im,
                          e_im[s:s + 1] + w_re * f_im + w_im * f_re)
        carry_ref[:, re_sl] = f_re
        carry_ref[:, im_sl] = f_im
        g_re, g_im = jnp.concatenate(enter_re, axis=0), jnp.concatenate(enter_im, axis=0)
        for j in range(SSM_SEG):
            rows = pl.ds(j * SUBLANES, SUBLANES)
            p_re, p_im = pre_ref[j:j + 1, re_sl], pim_ref[j:j + 1, re_sl]
            h_ref[b, rows, :] += p_re * g_re - p_im * g_im
            h_ref[n_blk + b, rows, :] += p_re * g_im + p_im * g_re

    h_all = jnp.concatenate([h_ref[b] for b in range(2 * n_blk)], axis=1)
    o_ref[...] = _ssm_out(h_all, u, cmat_ref, d_ref, wglu_ref)
    hre_ref[...] = carry_ref[:, :n]
    him_ref[...] = carry_ref[:, n:]


def ssm_prompt(u, tables, d, w_glu):
    t = u.shape[0]
    bmat, cmat, pre, pim = tables
    n = SSM_LANES
    n_chunks = t // SSM_CHUNK
    to_steps = lambda a: a.reshape(n_chunks, SUBLANES, SSM_SEG, -1).swapaxes(1, 2).reshape(t, -1)
    to_time = lambda a: a.reshape(n_chunks, SSM_SEG, SUBLANES, -1).swapaxes(1, 2).reshape(t, -1)
    ins = [to_steps(u), bmat.astype(BF16), cmat.astype(BF16), pre, pim, d.reshape(1, -1), w_glu.astype(BF16)]
    o, h_re, h_im = pl.pallas_call(
        _ssm_prompt_kernel,
        grid=(t // SSM_CHUNK,),
        in_specs=[pl.BlockSpec((SSM_CHUNK, SSM_W), lambda i: (i, 0))] + [_full(a.shape) for a in ins[1:]],
        out_specs=[pl.BlockSpec((SSM_CHUNK, SSM_W), lambda i: (i, 0)),
                   pl.BlockSpec((1, n), lambda i: (0, 0)), pl.BlockSpec((1, n), lambda i: (0, 0))],
        out_shape=[jax.ShapeDtypeStruct((t, SSM_W), F32),
                   jax.ShapeDtypeStruct((1, n), F32), jax.ShapeDtypeStruct((1, n), F32)],
        scratch_shapes=[pltpu.VMEM((2 * n // LANES, SSM_CHUNK, LANES), F32)] * 2 + [pltpu.VMEM((1, 2 * n), F32)],
        compiler_params=_params("arbitrary"),
    )(*ins)
    return to_time(o), h_re, h_im


def _ssm_sample_kernel(u_ref, h0re_ref, h0im_ref, bmat_ref, cmat_ref, pre_ref, pim_ref, d_ref, wglu_ref,
                       o_ref, hre_ref, him_ref):
    n = SSM_LANES
    u = u_ref[...]
    bu = jnp.dot(u, bmat_ref[...], preferred_element_type=F32, precision=lax.Precision.HIGHEST)
    a_re, a_im = pre_ref[0:1, :], pim_ref[0:1, :]
    h0_re, h0_im = h0re_ref[...], h0im_ref[...]
    h_re = bu[:, :n] + a_re * h0_re - a_im * h0_im
    h_im = bu[:, n:] + a_re * h0_im + a_im * h0_re
    hre_ref[...] = h_re
    him_ref[...] = h_im
    o_ref[...] = _ssm_out(jnp.concatenate([h_re, h_im], axis=1), u, cmat_ref, d_ref, wglu_ref)


def ssm_sample(u, h0_re, h0_im, tables, d, w_glu):
    b = u.shape[0]
    bmat, cmat, pre, pim = tables
    n = SSM_LANES
    return pl.pallas_call(
        _ssm_sample_kernel,
        out_shape=[jax.ShapeDtypeStruct((b, SSM_W), F32),
                   jax.ShapeDtypeStruct((b, n), F32), jax.ShapeDtypeStruct((b, n), F32)],
        compiler_params=pltpu.CompilerParams(vmem_limit_bytes=VMEM_LIMIT),
    )(u, h0_re.reshape(b, n), h0_im.reshape(b, n), bmat, cmat.astype(BF16), pre, pim,
      d.reshape(1, -1), w_glu.astype(BF16))


def _merge_kernel(x_ref, gate_ref, att_ref, ssm_ref, mem_ref, wa_ref, ws_ref, wm_ref, wo_ref,
                  nf_ref, wq_ref, skt_ref, h_out, hnt_out, st_out):
    g = gate_ref[...]
    mixed = (g[:, :D_MODEL] * _dot(att_ref[...].astype(BF16), wa_ref[...])
             + g[:, D_MODEL:2 * D_MODEL] * _dot(ssm_ref[...].astype(BF16), ws_ref[...])
             + g[:, 2 * D_MODEL:] * _dot(mem_ref[...].astype(BF16), wm_ref[...]))
    h = x_ref[...] + _dot(mixed.astype(BF16), wo_ref[...])
    hn = _rms(h, nf_ref[...], D_MODEL)
    pq = _dot(hn.astype(BF16), wq_ref[...]).astype(BF16)
    h_out[...] = h
    hnt_out[...] = hn.T.astype(BF16)
    st_out[...] = _dot_nt(skt_ref[...], pq)


def _sub_key_matrix(sub_keys):
    nb = PEER_HEADS * 2
    sk = sub_keys.reshape(nb, N_KEYS, PEER_DK // 2)
    eye = jnp.eye(nb, dtype=sk.dtype)
    return jnp.einsum('bnd,bc->bncd', sk, eye).reshape(nb * N_KEYS, nb * (PEER_DK // 2))


def merge(x, gate, att_o, ssm_o, mem_o, w_br_attn, w_br_ssm, w_br_mem, w_out, norm_ffn, peer_w_q,
          sub_keys, tile=256):
    t = x.shape[0]
    ws = [w.astype(BF16) for w in (w_br_attn, w_br_ssm, w_br_mem, w_out)]
    ins = [x, gate, att_o, ssm_o, mem_o, *ws, norm_ffn.reshape(1, -1), peer_w_q.astype(BF16),
           _sub_key_matrix(sub_keys).astype(BF16)]
    row = lambda w: pl.BlockSpec((tile, w), lambda i: (i, 0))
    n_scores = PEER_HEADS * 2 * N_KEYS
    return pl.pallas_call(
        _merge_kernel,
        grid=(t // tile,),
        in_specs=[row(D_MODEL), row(N_BRANCH * D_MODEL), row(ATT_W), row(SSM_W), row(MEM_W)]
        + [_full(a.shape) for a in ins[5:]],
        out_specs=[row(D_MODEL), pl.BlockSpec((D_MODEL, tile), lambda i: (0, i)),
                   pl.BlockSpec((n_scores, tile), lambda i: (0, i))],
        out_shape=[jax.ShapeDtypeStruct((t, D_MODEL), F32), jax.ShapeDtypeStruct((D_MODEL, t), BF16),
                   jax.ShapeDtypeStruct((n_scores, t), F32)],
        compiler_params=_params("parallel"),
    )(*ins)


PEER_TOKENS = 256
PEER_CHUNK = 2048
NEG_INF = float("-inf")


def _top_values(x, k):
    vals = []
    for _ in range(k):
        m = jnp.max(x, axis=0, keepdims=True)
        vals.append(m)
        x = jnp.where(x == m, NEG_INF, x)
    return vals


def _peer_route(st_ref, cut_ref, s2_ref, e2_ref, f1_ref):
    tokens = st_ref.shape[1]
    for h in range(PEER_HEADS):
        s1 = st_ref[pl.ds(2 * h * N_KEYS, N_KEYS), :]
        s2 = st_ref[pl.ds((2 * h + 1) * N_KEYS, N_KEYS), :]
        a = _top_values(s1, PEER_TOPK)
        b = _top_values(s2, PEER_TOPK)
        cand = [a[i] + b[j] for i in range(PEER_TOPK) for j in range(PEER_TOPK) if (i + 1) * (j + 1) <= PEER_TOPK]
        pad = -len(cand) % SUBLANES
        cand = jnp.concatenate(cand + [jnp.full((pad, tokens), NEG_INF, F32)], axis=0)
        thr = _top_values(cand, PEER_TOPK)[-1]
        top = a[0] + b[0]
        z = jnp.sum(jnp.where(cand >= thr, jnp.exp(cand - top), 0.0), axis=0, keepdims=True)
        cut = jnp.full(s1.shape, jnp.inf, F32)
        for bl in b:
            cut = jnp.where(s1 + bl >= thr, bl, cut)
        cut_ref[h] = cut
        s2_ref[h] = s2
        e2_ref[h] = jnp.exp(s2 - b[0])
        f1_ref[h] = jnp.exp(s1 - a[0]) / z


def _peer_kernel(hnt_ref, st_ref, h_ref, u_ref, vt_ref, y_ref,
                 cut_ref, s2_ref, e2_ref, f1_ref, acc_ref, pt_ref):
    e = pl.program_id(1)
    tokens = hnt_ref.shape[1]
    rows_per_step = PEER_CHUNK // N_KEYS

    @pl.when(e == 0)
    def _():
        _peer_route(st_ref, cut_ref, s2_ref, e2_ref, f1_ref)
        acc_ref[...] = jnp.zeros(acc_ref.shape, F32)

    for il in range(rows_per_step):
        i = e * rows_per_step + il
        sl = slice(il * N_KEYS, (il + 1) * N_KEYS)
        w = jnp.zeros((N_KEYS, tokens), F32)
        for h in range(PEER_HEADS):
            val = e2_ref[h] * f1_ref[h, pl.ds(i, 1), :]
            w = w + jnp.where(s2_ref[h] >= cut_ref[h, pl.ds(i, 1), :], val, 0.0)
        act = jax.nn.gelu(_dot(u_ref[sl, :], hnt_ref[...]))
        pt_ref[sl, :] = (w * act).astype(BF16)
    acc_ref[...] += _dot(vt_ref[...], pt_ref[...])

    @pl.when(e == pl.num_programs(1) - 1)
    def _():
        y_ref[...] = h_ref[...] + acc_ref[...].T


def peer(hnt, st, h, u_bf, vt_bf):
    t = hnt.shape[1]
    tb = PEER_TOKENS
    n_scores = st.shape[0]
    return pl.pallas_call(
        _peer_kernel,
        grid=(t // tb, N_EXPERTS // PEER_CHUNK),
        in_specs=[pl.BlockSpec((D_MODEL, tb), lambda i, e: (0, i)),
                  pl.BlockSpec((n_scores, tb), lambda i, e: (0, i)),
                  pl.BlockSpec((tb, D_MODEL), lambda i, e: (i, 0)),
                  pl.BlockSpec((PEER_CHUNK, D_MODEL), lambda i, e: (e, 0)),
                  pl.BlockSpec((D_MODEL, PEER_CHUNK), lambda i, e: (0, e))],
        out_specs=pl.BlockSpec((tb, D_MODEL), lambda i, e: (i, 0)),
        out_shape=jax.ShapeDtypeStruct((t, D_MODEL), F32),
        scratch_shapes=[pltpu.VMEM((PEER_HEADS, N_KEYS, tb), F32)] * 4
        + [pltpu.VMEM((D_MODEL, tb), F32), pltpu.VMEM((PEER_CHUNK, tb), BF16)],
        compiler_params=_params("parallel", "arbitrary"),
    )(hnt, st, h, u_bf, vt_bf)


def _bf16_round(x):
    return x.astype(BF16).astype(F32)


def _attend_rows(q, kb, vb):
    s = jnp.sum(_bf16_round(kb) * _bf16_round(q)[None], axis=2, keepdims=True)
    p = jnp.exp(s - jnp.max(s, axis=0, keepdims=True))
    p = p / jnp.sum(p, axis=0, keepdims=True)
    return jnp.sum(_bf16_round(p) * _bf16_round(vb), axis=0)


def _mem_sample_kernel(qm_ref, mk_ref, mv_ref, o_ref):
    o_ref[0] = _attend_rows(qm_ref[0] * (MEM_HD ** -0.5), mk_ref[0], mv_ref[0])


def mem_attend_sample(qm, cache_mem_k, cache_mem_v):
    b = qm.shape[0]
    blk = pl.BlockSpec((1, MEM_HEADS, MEM_HD), lambda i: (i, 0, 0))
    cache = pl.BlockSpec((1, MEM_TOKENS, MEM_HEADS, MEM_HD), lambda i: (i, 0, 0, 0))
    return pl.pallas_call(
        _mem_sample_kernel, grid=(b,), in_specs=[blk, cache, cache], out_specs=blk,
        out_shape=jax.ShapeDtypeStruct((b, MEM_HEADS, MEM_HD), F32),
        compiler_params=_params("parallel"),
    )(qm.reshape(b, MEM_HEADS, MEM_HD), cache_mem_k, cache_mem_v)


PAGE_GROUP = 8
RING_SLOTS = 8
SCORE_ROWS = 256


def _select_tile(keys, topk):
    rows = keys.shape[0]

    def count_ge(cand):
        return jnp.sum(jnp.sum(jnp.where(keys >= cand, 1.0, 0.0), axis=0, keepdims=True), axis=1, keepdims=True)

    n_valid = count_ge(jnp.full((1, 1), INT_MIN + 1, I32))
    thr = jnp.maximum(_kth_largest_key(count_ge, n_valid, float(topk)), INT_MIN + 1)
    need = float(topk) - count_ge(thr + 1)
    tie = jnp.where(keys == thr, 1.0, 0.0)
    ri = lax.broadcasted_iota(I32, (LANES, LANES), 0)
    ci = lax.broadcasted_iota(I32, (LANES, LANES), 1)
    within = _dot(tie.astype(BF16), jnp.where(ri <= ci, 1.0, 0.0).astype(BF16))
    row_tot = jnp.broadcast_to(jnp.sum(tie, axis=1, keepdims=True), (rows, LANES))
    rr = lax.broadcasted_iota(I32, (rows, rows), 0)
    rc = lax.broadcasted_iota(I32, (rows, rows), 1)
    before = _dot(jnp.where(rc < rr, 1.0, 0.0).astype(BF16), row_tot.astype(BF16))
    take_tie = jnp.where(within + before <= need, tie, 0.0)
    return jnp.where(keys > thr, 1.0, take_tie)


def _sample_attn_kernel(pt_ref, q_ref, qi_ref, wi_ref, kin_ref, kn_ref, vn_ref, cki_ref, ck_ref, cv_ref,
                        o_ref, kibuf, ring, qb_ref, key_ref, bias_ref, satt_ref, acc_ref,
                        sem_ki, sem_ring, *, topk):
    b, nb = pl.program_id(0), pl.num_programs(0)
    n_pages = pt_ref.shape[1]
    groups = n_pages // PAGE_GROUP
    n_stream = nb * 2 * groups
    par = b % 2

    def ki_copy(bb, buf, j):
        return pltpu.make_async_copy(cki_ref.at[pt_ref[bb, j]], kibuf.at[buf, j], sem_ki.at[buf])

    def start_ki(bb, buf):
        lax.fori_loop(0, n_pages, lambda j, c: (ki_copy(bb, buf, j).start(), c)[1], 0)

    def group_copy(cache_ref, bb, g, slot, i):
        return pltpu.make_async_copy(cache_ref.at[pt_ref[bb, g * PAGE_GROUP + i]], ring.at[slot, i],
                                     sem_ring.at[slot])

    def start_group(n):
        bb, g, slot = n // (2 * groups), n % (2 * groups), n % RING_SLOTS

        @pl.when(g < groups)
        def _():
            for i in range(PAGE_GROUP):
                group_copy(ck_ref, bb, g, slot, i).start()

        @pl.when(g >= groups)
        def _():
            for i in range(PAGE_GROUP):
                group_copy(cv_ref, bb, g - groups, slot, i).start()

    def next_group(n):
        slot = n % RING_SLOTS
        for i in range(PAGE_GROUP):
            group_copy(ck_ref, 0, 0, slot, i).wait()

        @pl.when(n + RING_SLOTS - 1 < n_stream)
        def _():
            start_group(n + RING_SLOTS - 1)

        return slot

    @pl.when(b == 0)
    def _():
        start_ki(0, 0)
        for n in range(RING_SLOTS - 1):
            start_group(n)

    @pl.when(b + 1 < nb)
    def _():
        start_ki(b + 1, 1 - par)

    lax.fori_loop(0, n_pages, lambda j, c: (ki_copy(b, par, j).wait(), c)[1], 0)

    lanes = (IDX_DIM, LANES)
    for h in range(ATT_HEADS):
        qb_ref[h] = jnp.broadcast_to(q_ref[0, h], lanes)
    qi_bf = qi_ref[0].astype(BF16)
    w = wi_ref[0]
    lane = lax.broadcasted_iota(I32, (1, LANES), 1)

    def head_dots(page_of_head, qsrc):
        return jnp.concatenate([jnp.sum(page_of_head(h) * qsrc[h], axis=0, keepdims=True)
                                for h in range(ATT_HEADS)], axis=0)

    def index_row(page):
        s = _dot(qi_bf, page.astype(BF16))
        return _sortable(jnp.sum(w * jnp.maximum(s, 0.0), axis=0, keepdims=True))

    key_ref[...] = jnp.full(key_ref.shape, INT_MIN, I32)

    def score_pages(g, c):
        j0 = pl.multiple_of(g * SUBLANES, SUBLANES)
        rows = [index_row(kibuf[par, j0 + i]) for i in range(SUBLANES)]
        key_ref[pl.ds(j0, SUBLANES), :] = jnp.concatenate(rows, axis=0)
        return c

    lax.fori_loop(0, n_pages // SUBLANES, score_pages, 0)
    new_row = index_row(jnp.broadcast_to(kin_ref[0], lanes))
    key_ref[pl.ds(n_pages, 1), :] = jnp.where(lane == 0, new_row, INT_MIN)
    bias_ref[...] = jnp.where(_select_tile(key_ref[...], topk) > 0.0, 0.0, MASKED_SCORE)

    def att_row(page, j):
        satt_ref[j] = head_dots(lambda h: page[h], qb_ref) + bias_ref[pl.ds(j, 1), :]

    def k_group(g, c):
        slot = next_group(b * 2 * groups + g)
        for i in range(PAGE_GROUP):
            att_row(ring[slot, i], g * PAGE_GROUP + i)
        return c

    lax.fori_loop(0, groups, k_group, 0)
    k_new = [jnp.broadcast_to(kn_ref[0, h], lanes) for h in range(ATT_HEADS)]
    att_row(k_new, n_pages)

    m = lax.fori_loop(0, n_pages + 1, lambda j, m: jnp.maximum(m, satt_ref[j]),
                      jnp.full((ATT_HEADS, LANES), MASKED_SCORE, F32))
    m = jnp.max(m, axis=1, keepdims=True)

    acc_ref[...] = jnp.zeros(acc_ref.shape, F32)

    def add_page(page, j, psum):
        p = jnp.exp(satt_ref[j] - m)
        for h in range(ATT_HEADS):
            acc_ref[h] += page[h] * p[h:h + 1, :]
        return psum + p

    def v_group(g, psum):
        slot = next_group(b * 2 * groups + groups + g)
        for i in range(PAGE_GROUP):
            psum = add_page(ring[slot, i], g * PAGE_GROUP + i, psum)
        return psum

    psum = lax.fori_loop(0, groups, v_group, jnp.zeros((ATT_HEADS, LANES), F32))
    v_new = [jnp.broadcast_to(vn_ref[0, h], lanes) for h in range(ATT_HEADS)]
    psum = add_page(v_new, n_pages, psum)
    denom = jnp.sum(psum, axis=1, keepdims=True)
    for h in range(ATT_HEADS):
        o_ref[0, h] = jnp.sum(acc_ref[h], axis=1, keepdims=True) / denom[h:h + 1, :]


def sample_attention(q, qi, wi, ki_new, k_new, v_new, cache_idx_k, cache_k, cache_v, page_table, topk):
    b = q.shape[0]
    n_pages = page_table.shape[1]
    col = lambda a, h: a.reshape(b, h, -1, 1)
    cki = jnp.transpose(cache_idx_k, (0, 2, 1))
    ck = jnp.transpose(cache_k, (0, 2, 3, 1))
    cv = jnp.transpose(cache_v, (0, 2, 3, 1))
    head_col = pl.BlockSpec((1, ATT_HEADS, HEAD_DIM, 1), lambda i, pt: (i, 0, 0, 0))
    any_spec = pl.BlockSpec(memory_space=pl.ANY)
    grid_spec = pltpu.PrefetchScalarGridSpec(
        num_scalar_prefetch=1, grid=(b,),
        in_specs=[head_col, pl.BlockSpec((1, IDX_HEADS, IDX_DIM), lambda i, pt: (i, 0, 0)),
                  pl.BlockSpec((1, IDX_HEADS, 1), lambda i, pt: (i, 0, 0)),
                  pl.BlockSpec((1, IDX_DIM, 1), lambda i, pt: (i, 0, 0)),
                  head_col, head_col, any_spec, any_spec, any_spec],
        out_specs=head_col,
        scratch_shapes=[pltpu.VMEM((2, n_pages, IDX_DIM, PAGE_SIZE), F32),
                        pltpu.VMEM((RING_SLOTS, PAGE_GROUP, ATT_HEADS, HEAD_DIM, PAGE_SIZE), F32),
                        pltpu.VMEM((ATT_HEADS, HEAD_DIM, LANES), F32),
                        pltpu.VMEM((SCORE_ROWS, LANES), I32),
                        pltpu.VMEM((SCORE_ROWS, LANES), F32),
                        pltpu.VMEM((n_pages + 1, ATT_HEADS, LANES), F32),
                        pltpu.VMEM((ATT_HEADS, HEAD_DIM, LANES), F32),
                        pltpu.SemaphoreType.DMA((2,)), pltpu.SemaphoreType.DMA((RING_SLOTS,))])
    out = pl.pallas_call(
        functools.partial(_sample_attn_kernel, topk=topk), grid_spec=grid_spec,
        out_shape=jax.ShapeDtypeStruct((b, ATT_HEADS, HEAD_DIM, 1), F32),
        compiler_params=_params("arbitrary"),
    )(page_table, col(q, ATT_HEADS), qi.reshape(b, IDX_HEADS, IDX_DIM), wi[:, :IDX_HEADS].reshape(b, IDX_HEADS, 1),
      ki_new.reshape(b, IDX_DIM, 1), col(k_new, ATT_HEADS), col(v_new, ATT_HEADS), cki, ck, cv)
    return out.reshape(b, ATT_W)


def _pad_rows(a, rows):
    return jnp.pad(a, ((0, rows - a.shape[0]), (0, 0)))


def kernel(x_prompt, x_sample, cache_k, cache_v, cache_idx_k, cache_mem_k, cache_mem_v, state_ssm_re, state_ssm_im, page_table, mem_prompt, norm_mix, w_in, q_norm, k_norm, idx_k_norm, mq_norm, ssm_a_re, ssm_a_im, ssm_b_re, ssm_b_im, ssm_c_re, ssm_c_im, ssm_d, ssm_log_dt, w_glu, mem_norm, w_mem_kv, mk_norm, w_br_attn, w_br_ssm, w_br_mem, w_out, norm_ffn, peer_w_q, peer_sub_keys, peer_u, peer_v):
    seq = x_prompt.shape[1]
    bd = x_sample.shape[0]
    w_parts = _split_w_in(w_in)
    tables = ssm_tables(ssm_a_re, ssm_a_im, ssm_b_re, ssm_b_im, ssm_c_re, ssm_c_im, ssm_log_dt)
    u_bf = peer_u.astype(BF16)
    vt_bf = peer_v.T.astype(BF16)
    tail = (w_br_attn, w_br_ssm, w_br_mem, w_out, norm_ffn, peer_w_q, peer_sub_keys)
    norms = (q_norm, k_norm, idx_k_norm, mq_norm)

    mk, mv = mem_kv(mem_prompt[0], mem_norm, w_mem_kv, mk_norm)
    (k_p, v_p, ki_p, wi, u, gate, q_ext, qi_ext, k_bf, v_bf, ki_dup, mem_o) = project(
        x_prompt[0], jnp.arange(seq), norm_mix, w_parts, *norms, mem=(mk, mv))
    att_o = attn_prompt(qi_ext, q_ext, wi, ki_dup, k_bf, v_bf)
    ssm_o, hre_p, him_p = ssm_prompt(u, tables, ssm_d, w_glu)
    h, hn, st = merge(x_prompt[0], gate, att_o, ssm_o, mem_o, *tail)
    y_prompt = peer(hn, st, h, u_bf, vt_bf)

    (k_s, v_s, ki_s, wi, u, gate, q, qi, qm) = project(
        x_sample[:, 0], jnp.full((bd,), PAST_LEN), norm_mix, w_parts, *norms)
    att_o = sample_attention(q, qi, wi, ki_s, k_s, v_s, cache_idx_k, cache_k, cache_v, page_table,
                             min(IDX_TOPK_MAX, (PAST_LEN + 1) // 4))
    ssm_o, hre_s, him_s = ssm_sample(u, state_ssm_re, state_ssm_im, tables, ssm_d, w_glu)
    mem_o = mem_attend_sample(qm, cache_mem_k, cache_mem_v).reshape(bd, MEM_W)
    pad = lambda a: _pad_rows(a, PEER_TOKENS)
    h, hn, st = merge(pad(x_sample[:, 0]), pad(gate), pad(att_o), pad(ssm_o), pad(mem_o), *tail)
    y_sample = peer(hn, st, h, u_bf, vt_bf)[:bd]

    heads = lambda a: a.reshape(a.shape[0], ATT_HEADS, HEAD_DIM)
    state = lambda a, n: a.reshape(n, SSM_GROUPS, SSM_STATE)
    return (y_prompt[None], y_sample[:, None],
            heads(k_p)[None], heads(v_p)[None], ki_p[None],
            mk.reshape(1, MEM_TOKENS, MEM_HEADS, MEM_HD), mv.reshape(1, MEM_TOKENS, MEM_HEADS, MEM_HD),
            state(hre_p, 1), state(him_p, 1),
            heads(k_s)[:, None], heads(v_s)[:, None], ki_s[:, None],
            state(hre_s, bd), state(him_s, bd))
```
